```python
import jax
import jax.numpy as jnp
from jax import lax
import numpy as np


D_MODEL = 1024
BATCH = 16
SEQ = 2048
DEPTH = 1

SB_HEADS = 16
SB_HEAD_DIM = 64
SB_WIDTH = SB_HEADS * SB_HEAD_DIM
RET_HEADS = 4
RET_QK_DIM = 256
RET_V_DIM = 2 * RET_QK_DIM
RET_QK_WIDTH = RET_HEADS * RET_QK_DIM
RET_V_WIDTH = RET_HEADS * RET_V_DIM
N_BRANCH = 2
BLOCK = 128
D_FF = -(-8 * D_MODEL // (3 * 256)) * 256
ROPE_BASE = 10000.0
EPS = 1e-6
N_MOD = 6
IN_SPLITS = (SB_WIDTH, 2 * SB_WIDTH, 3 * SB_WIDTH,
             3 * SB_WIDTH + RET_QK_WIDTH,
             3 * SB_WIDTH + 2 * RET_QK_WIDTH,
             3 * SB_WIDTH + 2 * RET_QK_WIDTH + RET_V_WIDTH,
             3 * SB_WIDTH + 2 * RET_QK_WIDTH + 2 * RET_V_WIDTH)
IN_WIDTH = 3 * SB_WIDTH + 2 * RET_QK_WIDTH + 2 * RET_V_WIDTH + N_BRANCH * D_MODEL

kernel_name = 'hybrid_stickbreaking_retention_block'


def rmsnorm(t, g):
    tf = t.astype(jnp.float32)
    y = tf * lax.rsqrt(jnp.mean(tf * tf, axis=-1, keepdims=True) + EPS)
    return (y * g.astype(jnp.float32)).astype(t.dtype)


def head_rmsnorm(t):
    tf = t.astype(jnp.float32)
    return (tf * lax.rsqrt(jnp.mean(tf * tf, axis=-1, keepdims=True) + EPS)).astype(t.dtype)


def modulate(h, shift, scale):
    return h * (1.0 + scale[:, None, :]) + shift[:, None, :]


def rotary(t):
    s, d = t.shape[1], t.shape[-1]
    inv_freq = ROPE_BASE ** (-jnp.arange(0, d, 2, dtype=jnp.float32) / d)
    ang = jnp.arange(s, dtype=jnp.float32)[:, None] * inv_freq[None, :]
    cos = jnp.cos(ang)[None, :, None, :]
    sin = jnp.sin(ang)[None, :, None, :]
    t1, t2 = jnp.split(t.astype(jnp.float32), 2, axis=-1)
    return jnp.concatenate([t1 * cos - t2 * sin, t1 * sin + t2 * cos], axis=-1).astype(t.dtype)


def stick_breaking_attention(q, k, v):
    s_len = q.shape[2]
    scale = SB_HEAD_DIM ** -0.5
    outs = []
    for i in range(s_len // BLOCK):
        end = (i + 1) * BLOCK
        qb = q[:, :, i * BLOCK:end]
        kb = k[:, :, :end]
        vb = v[:, :, :end]
        z = jnp.einsum('bhtd,bhsd->bhts', qb, kb).astype(jnp.float32) * scale
        t_pos = i * BLOCK + jnp.arange(BLOCK)
        s_pos = jnp.arange(end)
        mask = s_pos[None, :] < t_pos[:, None]
        log_1m_beta = jnp.where(mask, jax.nn.log_sigmoid(-z), 0.0)
        after = lax.cumsum(log_1m_beta, axis=3, reverse=True) - log_1m_beta
        log_a = jax.nn.log_sigmoid(z) + after
        a = jnp.where(mask, jnp.exp(log_a), 0.0)
        outs.append(jnp.einsum('bhts,bhsd->bhtd', a.astype(v.dtype), vb))
    return jnp.concatenate(outs, axis=2)


def retention(q, k, v):
    out_dtype = v.dtype
    q = q.astype(jnp.float32)
    k = k.astype(jnp.float32)
    v = v.astype(jnp.float32)
    b, h, s_len, dk = q.shape
    dv = v.shape[-1]
    n_chunks = s_len // BLOCK
    log_gamma = jnp.log(1.0 - 2.0 ** (-5.0 - jnp.arange(h, dtype=jnp.float32)))
    idx = jnp.arange(BLOCK, dtype=jnp.float32)
    diff = idx[:, None] - idx[None, :]
    decay_mask = jnp.where(diff >= 0, jnp.exp(jnp.maximum(diff, 0.0) * log_gamma[:, None, None]), 0.0)
    query_decay = jnp.exp((idx + 1.0) * log_gamma[:, None])
    key_decay = jnp.exp((BLOCK - 1.0 - idx) * log_gamma[:, None])
    chunk_decay = jnp.exp(BLOCK * log_gamma)
    qc = q.reshape(b, h, n_chunks, BLOCK, dk)
    kc = k.reshape(b, h, n_chunks, BLOCK, dk)
    vc = v.reshape(b, h, n_chunks, BLOCK, dv)
    scores = jnp.einsum('bhncd,bhnmd->bhncm', qc, kc) * decay_mask[None, :, None]
    intra = jnp.einsum('bhncm,bhnme->bhnce', scores, vc)
    k_dec = kc * key_decay[None, :, None, :, None]

    def step(state, xs):
        q_i, k_i, v_i = xs
        cross = jnp.einsum('bhcd,bhde->bhce', q_i, state) * query_decay[None, :, :, None]
        state = state * chunk_decay[None, :, None, None] + jnp.einsum('bhcd,bhce->bhde', k_i, v_i)
        return state, cross

    xs = (jnp.moveaxis(qc, 2, 0), jnp.moveaxis(k_dec, 2, 0), jnp.moveaxis(vc, 2, 0))
    init = jnp.zeros((b, h, dk, dv), jnp.float32)
    _, cross = lax.scan(step, init, xs)
    out = intra + jnp.moveaxis(cross, 0, 2)
    return out.reshape(b, h, s_len, dv).astype(out_dtype)


def setup_inputs(seed: int = 0) -> dict:
    key = jax.random.key(seed)
    ks = jax.random.split(key, 14)

    def nrm(k, shape, fan_in):
        return jax.random.normal(k, shape, jnp.float32) * (fan_in ** -0.5)

    x = jax.random.normal(ks[0], (BATCH, SEQ, D_MODEL), jnp.float32)
    c = jax.random.normal(ks[1], (BATCH, D_MODEL), jnp.float32)
    w_ada = nrm(ks[2], (DEPTH, D_MODEL, N_MOD * D_MODEL), D_MODEL)
    b_ada = 0.02 * jax.random.normal(ks[3], (DEPTH, N_MOD * D_MODEL), jnp.float32)
    g_norm1 = 1.0 + 0.02 * jax.random.normal(ks[4], (DEPTH, D_MODEL), jnp.float32)
    w_in = nrm(ks[5], (DEPTH, D_MODEL, IN_WIDTH), D_MODEL)
    b_branch = 0.02 * jax.random.normal(ks[6], (DEPTH, N_BRANCH * D_MODEL), jnp.float32)
    w_proj_sb = nrm(ks[7], (DEPTH, SB_WIDTH, D_MODEL), SB_WIDTH)
    w_proj_ret = nrm(ks[8], (DEPTH, RET_V_WIDTH, D_MODEL), RET_V_WIDTH)
    w_out = nrm(ks[9], (DEPTH, D_MODEL, D_MODEL), D_MODEL)
    g_norm2 = 1.0 + 0.02 * jax.random.normal(ks[10], (DEPTH, D_MODEL), jnp.float32)
    w_ffn_in = nrm(ks[11], (DEPTH, D_MODEL, 2 * D_FF), D_MODEL)
    w_ffn_out = nrm(ks[12], (DEPTH, D_FF, D_MODEL), D_FF)
    g_final = 1.0 + 0.02 * jax.random.normal(ks[13], (D_MODEL,), jnp.float32)
    return {'x': x, 'c': c, 'w_ada': w_ada, 'b_ada': b_ada, 'g_norm1': g_norm1,
            'w_in': w_in, 'b_branch': b_branch, 'w_proj_sb': w_proj_sb,
            'w_proj_ret': w_proj_ret, 'w_out': w_out, 'g_norm2': g_norm2,
            'w_ffn_in': w_ffn_in, 'w_ffn_out': w_ffn_out, 'g_final': g_final}


def reference(x, c, w_ada, b_ada, g_norm1, w_in, b_branch, w_proj_sb, w_proj_ret,
              w_out, g_norm2, w_ffn_in, w_ffn_out, g_final):
    b, s_len, _ = x.shape
    c_act = jax.nn.silu(c)
    for l in range(DEPTH):
        mod = jnp.einsum('bd,de->be', c_act, w_ada[l]) + b_ada[l]
        shift1, scale1, gate1, shift2, scale2, gate2 = jnp.split(mod, N_MOD, axis=-1)

        hmix = modulate(rmsnorm(x, g_norm1[l]), shift1, scale1)
        proj = jnp.einsum('bsd,de->bse', hmix, w_in[l])
        sb_q, sb_k, sb_v, r_q, r_k, r_v, r_g, br = jnp.split(proj, IN_SPLITS, axis=-1)

        def sb_heads(t):
            return t.reshape(b, s_len, SB_HEADS, SB_HEAD_DIM).transpose(0, 2, 1, 3)
        o_sb = stick_breaking_attention(sb_heads(sb_q), sb_heads(sb_k), sb_heads(sb_v))
        o_sb = o_sb.transpose(0, 2, 1, 3).reshape(b, s_len, SB_WIDTH)

        rq = rotary(r_q.reshape(b, s_len, RET_HEADS, RET_QK_DIM))
        rk = rotary(r_k.reshape(b, s_len, RET_HEADS, RET_QK_DIM)) * (RET_QK_DIM ** -0.5)
        rv = r_v.reshape(b, s_len, RET_HEADS, RET_V_DIM)
        o_ret = retention(rq.transpose(0, 2, 1, 3), rk.transpose(0, 2, 1, 3), rv.transpose(0, 2, 1, 3))
        o_ret = head_rmsnorm(o_ret.transpose(0, 2, 1, 3)).reshape(b, s_len, RET_V_WIDTH)
        o_ret = jax.nn.silu(r_g) * o_ret

        g_sb, g_ret = jnp.split(jax.nn.sigmoid(br + b_branch[l]), N_BRANCH, axis=-1)
        merged = (g_sb * jnp.einsum('bse,ed->bsd', o_sb, w_proj_sb[l])
                  + g_ret * jnp.einsum('bse,ed->bsd', o_ret, w_proj_ret[l]))
        x = x + gate1[:, None, :] * jnp.einsum('bsd,de->bse', merged, w_out[l])

        hffn = modulate(rmsnorm(x, g_norm2[l]), shift2, scale2)
        a, u = jnp.split(jnp.einsum('bsd,df->bsf', hffn, w_ffn_in[l]), 2, axis=-1)
        x = x + gate2[:, None, :] * jnp.einsum('bsf,fd->bsd', jax.nn.silu(a) * u, w_ffn_out[l])

    return rmsnorm(x, g_final)
```

```python
import functools

import jax
import jax.numpy as jnp
from jax import lax
from jax.experimental import pallas as pl
from jax.experimental.pallas import tpu as pltpu

D_MODEL = 1024
SB_HEADS = 16
SB_HEAD_DIM = 64
SB_WIDTH = SB_HEADS * SB_HEAD_DIM
RET_HEADS = 4
RET_QK_DIM = 256
RET_V_DIM = 512
RET_QK_WIDTH = RET_HEADS * RET_QK_DIM
RET_V_WIDTH = RET_HEADS * RET_V_DIM
D_FF = 2816
ROPE_BASE = 10000.0
EPS = 1e-6
N_MOD = 6

LANE = 128
OFF_SB_Q = 0
OFF_SB_K = SB_WIDTH
OFF_SB_V = 2 * SB_WIDTH
OFF_R_Q = 3 * SB_WIDTH
OFF_R_K = OFF_R_Q + RET_QK_WIDTH
OFF_R_V = OFF_R_K + RET_QK_WIDTH
OFF_R_G = OFF_R_V + RET_V_WIDTH
OFF_BR = OFF_R_G + RET_V_WIDTH
IN_WIDTH = OFF_BR + 2 * D_MODEL

F32 = jnp.float32
BF16 = jnp.bfloat16

VMEM_LIMIT = 56 * 1024 * 1024

TM_IN = 1024
TN_IN = 1024
SB_TQ = 256
SB_TK = 256
RET_C = 256
TM_OUT = 512


def _cparams(sem):
    return pltpu.CompilerParams(dimension_semantics=sem, vmem_limit_bytes=VMEM_LIMIT)


def _const_spec(shape, index_map):
    return pl.BlockSpec(shape, index_map, pipeline_mode=pl.Buffered(1))


def _ada_kernel(c_ref, w_ref, b_ref, o_ref):
    c = c_ref[...]
    c_act = (c * jax.nn.sigmoid(c)).astype(BF16)
    o_ref[...] = jnp.dot(c_act, w_ref[...].astype(BF16),
                         preferred_element_type=F32) + b_ref[...]


def _ada(c, w_ada, b_ada):
    b, d = c.shape
    n = w_ada.shape[1]
    tn = 1024
    return pl.pallas_call(
        _ada_kernel,
        grid=(n // tn,),
        in_specs=[pl.BlockSpec((b, d), lambda j: (0, 0)),
                  pl.BlockSpec((d, tn), lambda j: (0, j)),
                  pl.BlockSpec((1, tn), lambda j: (0, j))],
        out_specs=pl.BlockSpec((b, tn), lambda j: (0, j)),
        out_shape=jax.ShapeDtypeStruct((b, n), F32),
        compiler_params=_cparams(("arbitrary",)),
        name="ada",
    )(c, w_ada, b_ada.reshape(1, n))


def _rope_kernel(cos_ref, sin_ref):
    s, half = cos_ref.shape
    i = lax.broadcasted_iota(jnp.int32, (s, half), 1).astype(F32)
    pos = lax.broadcasted_iota(jnp.int32, (s, half), 0).astype(F32)
    inv_freq = jnp.power(ROPE_BASE, -(2.0 * i) / RET_QK_DIM)
    ang = pos * inv_freq
    cos_ref[...] = jnp.cos(ang)
    sin_ref[...] = jnp.sin(ang)


def _rope_tables(s_len):
    half = RET_QK_DIM // 2
    shp = jax.ShapeDtypeStruct((s_len, half), F32)
    return pl.pallas_call(
        _rope_kernel,
        out_shape=(shp, shp),
        compiler_params=pltpu.CompilerParams(vmem_limit_bytes=VMEM_LIMIT),
        name="rope",
    )()


def _inproj_kernel(x_ref, g_ref, shift_ref, scale_ref, w_ref, o_ref, h_ref):
    @pl.when(pl.program_id(2) == 0)
    def _():
        x = x_ref[0]
        y = x * lax.rsqrt(jnp.mean(x * x, axis=-1, keepdims=True) + EPS)
        y = y * g_ref[...]
        h = y * (1.0 + scale_ref[0, 0]) + shift_ref[0, 0]
        h_ref[...] = h.astype(BF16)

    o_ref[0] = jnp.dot(h_ref[...], w_ref[...], preferred_element_type=F32).astype(BF16)


def _inproj(x, g_norm1, mod4, w_in_bf16):
    b, s, d = x.shape
    n = w_in_bf16.shape[1]
    return pl.pallas_call(
        _inproj_kernel,
        grid=(b, s // TM_IN, n // TN_IN),
        in_specs=[pl.BlockSpec((1, TM_IN, d), lambda bi, si, ni: (bi, si, 0)),
                  pl.BlockSpec((1, d), lambda bi, si, ni: (0, 0)),
                  pl.BlockSpec((1, 1, 1, d), lambda bi, si, ni: (bi, 0, 0, 0)),
                  pl.BlockSpec((1, 1, 1, d), lambda bi, si, ni: (bi, 1, 0, 0)),
                  pl.BlockSpec((d, TN_IN), lambda bi, si, ni: (0, ni))],
        out_specs=pl.BlockSpec((1, TM_IN, TN_IN), lambda bi, si, ni: (bi, si, ni)),
        out_shape=jax.ShapeDtypeStruct((b, s, n), BF16),
        scratch_shapes=[pltpu.VMEM((TM_IN, d), BF16)],
        compiler_params=_cparams(("arbitrary", "arbitrary", "arbitrary")),
        name="inproj",
    )(x, g_norm1.reshape(1, d), mod4, mod4, w_in_bf16)


def _sb_kernel(q_ref, k_ref, v_ref, o_ref):
    qi = pl.program_id(2)
    tq, tk, hd = SB_TQ, SB_TK, SB_HEAD_DIM

    q = q_ref[0] * (SB_HEAD_DIM ** -0.5)
    lane_q = lax.broadcasted_iota(jnp.int32, (tq, LANE), 1)
    zq = jnp.zeros_like(q)
    qs = jnp.concatenate([jnp.where(lane_q < hd, q, zq),
                          jnp.where(lane_q >= hd, q, zq)], axis=0)

    ur = lax.broadcasted_iota(jnp.int32, (tk, tk), 0)
    uc = lax.broadcasted_iota(jnp.int32, (tk, tk), 1)
    u = jnp.where(ur >= uc, 1.0, 0.0).astype(BF16)
    uu = jnp.concatenate([u, u], axis=0)

    lane_v = lax.broadcasted_iota(jnp.int32, (tk, LANE), 1)
    row_t = lax.broadcasted_iota(jnp.int32, (2 * tq, tk), 0) & (tq - 1)
    col_s = lax.broadcasted_iota(jnp.int32, (2 * tq, tk), 1)
    causal = col_s < row_t

    def tile(kb, carry, acc, diag):
        start = pl.multiple_of(kb * tk, tk)
        k = k_ref[0, pl.ds(start, tk), :]
        v = v_ref[0, pl.ds(start, tk), :]
        z = lax.dot_general(qs, k, (((1,), (1,)), ((), ())),
                            preferred_element_type=F32)
        sp = jnp.maximum(z, 0.0) + jnp.log(1.0 + jnp.exp(-jnp.abs(z)))
        if diag:
            sp = jnp.where(causal, sp, 0.0)
        hi = sp.astype(BF16)
        lo = (sp - hi.astype(F32)).astype(BF16)
        cum = jnp.dot(jnp.concatenate([hi, lo], axis=1), uu,
                      preferred_element_type=F32)
        a = jnp.exp(z - carry - cum)
        if diag:
            a = jnp.where(causal, a, 0.0)
        a = a.astype(BF16)
        zv = jnp.zeros_like(v)
        vv = jnp.concatenate([jnp.where(lane_v < hd, v, zv),
                              jnp.where(lane_v >= hd, v, zv)], axis=0)
        aa = jnp.concatenate([a[:tq], a[tq:]], axis=1)
        acc = acc + jnp.dot(aa, vv, preferred_element_type=F32)
        carry = carry + jnp.sum(sp, axis=-1, keepdims=True)
        return carry, acc

    carry0 = jnp.zeros((2 * tq, 1), F32)
    acc0 = jnp.zeros((tq, LANE), F32)
    carry, acc = tile(qi, carry0, acc0, True)

    def body(j, c):
        return tile(qi - 1 - j, c[0], c[1], False)

    carry, acc = lax.fori_loop(0, qi, body, (carry, acc))
    o_ref[0] = acc.astype(BF16)


def _sb_attention(proj):
    b, s, _ = proj.shape
    n_pair = SB_WIDTH // LANE
    return pl.pallas_call(
        _sb_kernel,
        grid=(b, n_pair, s // SB_TQ),
        in_specs=[pl.BlockSpec((1, SB_TQ, LANE), lambda bi, hp, qi: (bi, qi, OFF_SB_Q // LANE + hp)),
                  pl.BlockSpec((1, s, LANE), lambda bi, hp, qi: (bi, 0, OFF_SB_K // LANE + hp)),
                  pl.BlockSpec((1, s, LANE), lambda bi, hp, qi: (bi, 0, OFF_SB_V // LANE + hp))],
        out_specs=pl.BlockSpec((1, SB_TQ, LANE), lambda bi, hp, qi: (bi, qi, hp)),
        out_shape=jax.ShapeDtypeStruct((b, s, SB_WIDTH), BF16),
        compiler_params=_cparams(("arbitrary", "arbitrary", "arbitrary")),
        name="sb",
    )(proj, proj, proj)


def _rotate(t, cos, sin):
    half = RET_QK_DIM // 2
    t1 = t[:, :half]
    t2 = t[:, half:]
    return jnp.concatenate([t1 * cos - t2 * sin, t1 * sin + t2 * cos], axis=-1)


def _ret_kernel(lg_ref, q_ref, k_ref, v_ref, g_ref, cos_ref, sin_ref, o_ref,
                state_ref, dmask_ref, qdec_ref, kdec_ref):
    h = pl.program_id(1)
    n = pl.program_id(2)
    c = RET_C
    log_gamma = lg_ref[h]

    @pl.when(n == 0)
    def _():
        state_ref[...] = jnp.zeros_like(state_ref)
        r = lax.broadcasted_iota(jnp.int32, (c, c), 0)
        cc = lax.broadcasted_iota(jnp.int32, (c, c), 1)
        diff = (r - cc).astype(F32)
        dmask_ref[...] = jnp.where(diff >= 0.0, jnp.exp(jnp.maximum(diff, 0.0) * log_gamma), 0.0)
        idx = lax.broadcasted_iota(jnp.int32, (c, 1), 0).astype(F32)
        qdec_ref[...] = jnp.exp((idx + 1.0) * log_gamma)
        kdec_ref[...] = jnp.exp((c - 1.0 - idx) * log_gamma)

    cos = cos_ref[...]
    sin = sin_ref[...]
    qr = _rotate(q_ref[0].astype(F32), cos, sin).astype(BF16)
    kr = _rotate(k_ref[0].astype(F32), cos, sin) * (RET_QK_DIM ** -0.5)
    kd = (kr * kdec_ref[...]).astype(BF16)
    kr = kr.astype(BF16)
    v = v_ref[0]

    scores = lax.dot_general(qr, kr, (((1,), (1,)), ((), ())),
                             preferred_element_type=F32) * dmask_ref[...]
    intra = jnp.dot(scores.astype(BF16), v, preferred_element_type=F32)
    state = state_ref[...]
    cross = jnp.dot(qr, state.astype(BF16), preferred_element_type=F32) * qdec_ref[...]
    chunk_decay = jnp.exp(jnp.full((1, RET_V_DIM), c * log_gamma, F32))
    state_ref[...] = state * chunk_decay + lax.dot_general(
        kd, v, (((0,), (0,)), ((), ())), preferred_element_type=F32)

    out = intra + cross
    out = out * lax.rsqrt(jnp.mean(out * out, axis=-1, keepdims=True) + EPS)
    g = g_ref[0].astype(F32)
    o_ref[0] = (g * jax.nn.sigmoid(g) * out).astype(BF16)


def _retention(proj, cos, sin, log_gamma):
    b, s, _ = proj.shape
    c = RET_C
    qk_blk = RET_QK_DIM
    v_blk = RET_V_DIM
    return pl.pallas_call(
        _ret_kernel,
        grid=(b, RET_HEADS, s // c),
        in_specs=[pl.BlockSpec(memory_space=pltpu.SMEM),
                  pl.BlockSpec((1, c, qk_blk), lambda bi, h, n: (bi, n, OFF_R_Q // qk_blk + h)),
                  pl.BlockSpec((1, c, qk_blk), lambda bi, h, n: (bi, n, OFF_R_K // qk_blk + h)),
                  pl.BlockSpec((1, c, v_blk), lambda bi, h, n: (bi, n, OFF_R_V // v_blk + h)),
                  pl.BlockSpec((1, c, v_blk), lambda bi, h, n: (bi, n, OFF_R_G // v_blk + h)),
                  pl.BlockSpec((c, RET_QK_DIM // 2), lambda bi, h, n: (n, 0)),
                  pl.BlockSpec((c, RET_QK_DIM // 2), lambda bi, h, n: (n, 0))],
        out_specs=pl.BlockSpec((1, c, v_blk), lambda bi, h, n: (bi, n, h)),
        out_shape=jax.ShapeDtypeStruct((b, s, RET_V_WIDTH), BF16),
        scratch_shapes=[pltpu.VMEM((RET_QK_DIM, RET_V_DIM), F32),
                        pltpu.VMEM((c, c), F32),
                        pltpu.VMEM((c, 1), F32),
                        pltpu.VMEM((c, 1), F32)],
        compiler_params=_cparams(("arbitrary", "arbitrary", "arbitrary")),
        name="ret",
    )(log_gamma, proj, proj, proj, proj, cos, sin)


def _merge_kernel(x_ref, osb_ref, oret_ref, brsb_ref, brret_ref, bb_ref, gate_ref,
                  wsb_ref, wret_ref, wout_ref, o_ref):
    d = D_MODEL
    p_sb = jnp.dot(osb_ref[0], wsb_ref[...], preferred_element_type=F32)
    p_ret = jnp.dot(oret_ref[0], wret_ref[...], preferred_element_type=F32)
    g_sb = jax.nn.sigmoid(brsb_ref[0].astype(F32) + bb_ref[:, :d])
    g_ret = jax.nn.sigmoid(brret_ref[0].astype(F32) + bb_ref[:, d:])
    merged = (g_sb * p_sb + g_ret * p_ret).astype(BF16)
    o_ref[0] = x_ref[0] + gate_ref[0, 0] * jnp.dot(merged, wout_ref[...],
                                                  preferred_element_type=F32)


def _merge(x, o_sb, o_ret, proj, b_branch, mod4, w_sb, w_ret, w_out):
    b, s, d = x.shape
    tm = TM_OUT
    tok = lambda bi, si: (bi, si, 0)
    zero2 = lambda bi, si: (0, 0)
    return pl.pallas_call(
        _merge_kernel,
        grid=(b, s // tm),
        in_specs=[pl.BlockSpec((1, tm, d), tok),
                  pl.BlockSpec((1, tm, SB_WIDTH), tok),
                  pl.BlockSpec((1, tm, RET_V_WIDTH), tok),
                  pl.BlockSpec((1, tm, d), lambda bi, si: (bi, si, OFF_BR // d)),
                  pl.BlockSpec((1, tm, d), lambda bi, si: (bi, si, OFF_BR // d + 1)),
                  pl.BlockSpec((1, 2 * d), zero2),
                  pl.BlockSpec((1, 1, 1, d), lambda bi, si: (bi, 2, 0, 0)),
                  _const_spec((SB_WIDTH, d), zero2),
                  _const_spec((RET_V_WIDTH, d), zero2),
                  _const_spec((d, d), zero2)],
        out_specs=pl.BlockSpec((1, tm, d), tok),
        out_shape=jax.ShapeDtypeStruct((b, s, d), F32),
        compiler_params=_cparams(("arbitrary", "arbitrary")),
        name="merge",
    )(x, o_sb, o_ret, proj, proj, b_branch.reshape(1, 2 * d), mod4, w_sb, w_ret, w_out)


def _ffn_kernel(x_ref, g2_ref, shift_ref, scale_ref, gate_ref, gf_ref, wa_ref, wu_ref, wo_ref, o_ref,
                *, final_norm):
    x = x_ref[0]
    y = x * lax.rsqrt(jnp.mean(x * x, axis=-1, keepdims=True) + EPS)
    y = y * g2_ref[...]
    h = (y * (1.0 + scale_ref[0, 0]) + shift_ref[0, 0]).astype(BF16)
    a = jnp.dot(h, wa_ref[...], preferred_element_type=F32)
    u = jnp.dot(h, wu_ref[...], preferred_element_type=F32)
    t = (a * jax.nn.sigmoid(a) * u).astype(BF16)
    x2 = x + gate_ref[0, 0] * jnp.dot(t, wo_ref[...], preferred_element_type=F32)
    if final_norm:
        r = x2 * lax.rsqrt(jnp.mean(x2 * x2, axis=-1, keepdims=True) + EPS)
        x2 = r * gf_ref[...]
    o_ref[0] = x2


def _ffn(x, g_norm2, mod4, g_final, w_a, w_u, w_o, final_norm):
    b, s, d = x.shape
    tm = TM_OUT
    tok = lambda bi, si: (bi, si, 0)
    zero2 = lambda bi, si: (0, 0)
    return pl.pallas_call(
        functools.partial(_ffn_kernel, final_norm=final_norm),
        grid=(b, s // tm),
        in_specs=[pl.BlockSpec((1, tm, d), tok),
                  pl.BlockSpec((1, d), zero2),
                  pl.BlockSpec((1, 1, 1, d), lambda bi, si: (bi, 3, 0, 0)),
                  pl.BlockSpec((1, 1, 1, d), lambda bi, si: (bi, 4, 0, 0)),
                  pl.BlockSpec((1, 1, 1, d), lambda bi, si: (bi, 5, 0, 0)),
                  pl.BlockSpec((1, d), zero2),
                  _const_spec((d, D_FF), zero2),
                  _const_spec((d, D_FF), zero2),
                  _const_spec((D_FF, d), zero2)],
        out_specs=pl.BlockSpec((1, tm, d), tok),
        out_shape=jax.ShapeDtypeStruct((b, s, d), F32),
        compiler_params=_cparams(("arbitrary", "arbitrary")),
        name="ffn",
    )(x, g_norm2.reshape(1, d), mod4, mod4, mod4, g_final.reshape(1, d), w_a, w_u, w_o)


def kernel(x, c, w_ada, b_ada, g_norm1, w_in, b_branch, w_proj_sb, w_proj_ret, w_out,
           g_norm2, w_ffn_in, w_ffn_out, g_final):
    b, s, d = x.shape
    depth = w_ada.shape[0]
    cos, sin = _rope_tables(s)
    log_gamma = jnp.log(1.0 - 2.0 ** (-5.0 - jnp.arange(RET_HEADS, dtype=jnp.float32)))
    for l in range(depth):
        mod4 = _ada(c, w_ada[l], b_ada[l]).reshape(b, N_MOD, 1, d)
        proj = _inproj(x, g_norm1[l], mod4, w_in[l].astype(BF16))
        o_sb = _sb_attention(proj)
        o_ret = _retention(proj, cos, sin, log_gamma)
        x = _merge(x, o_sb, o_ret, proj, b_branch[l], mod4,
                   w_proj_sb[l].astype(BF16), w_proj_ret[l].astype(BF16), w_out[l].astype(BF16))
        w_ffn = w_ffn_in[l].astype(BF16)
        x = _ffn(x, g_norm2[l], mod4, g_final, w_ffn[:, :D_FF], w_ffn[:, D_FF:],
                 w_ffn_out[l].astype(BF16), final_norm=(l == depth - 1))
    return x
```

```python
import functools

import jax
import jax.numpy as jnp
from jax import lax
from jax.experimental import pallas as pl
from jax.experimental.pallas import tpu as pltpu

D_MODEL = 1024
SB_HEADS = 16
SB_HEAD_DIM = 64
SB_WIDTH = SB_HEADS * SB_HEAD_DIM
RET_HEADS = 4
RET_QK_DIM = 256
RET_V_DIM = 512
RET_QK_WIDTH = RET_HEADS * RET_QK_DIM
RET_V_WIDTH = RET_HEADS * RET_V_DIM
D_FF = 2816
ROPE_BASE = 10000.0
EPS = 1e-6
N_MOD = 6

LANE = 128
OFF_SB_Q = 0
OFF_SB_K = SB_WIDTH
OFF_SB_V = 2 * SB_WIDTH
OFF_R_Q = 3 * SB_WIDTH
OFF_R_K = OFF_R_Q + RET_QK_WIDTH
OFF_R_V = OFF_R_K + RET_QK_WIDTH
OFF_R_G = OFF_R_V + RET_V_WIDTH
OFF_BR = OFF_R_G + RET_V_WIDTH
IN_WIDTH = OFF_BR + 2 * D_MODEL

F32 = jnp.float32
BF16 = jnp.bfloat16

VMEM_LIMIT = 56 * 1024 * 1024

TM_IN = 1024
TN_IN = 1024
SB_TQ = 256
SB_TK = 256
SB_G = 4
RET_C = 256
TM_OUT = 512


def _cparams(sem):
    return pltpu.CompilerParams(dimension_semantics=sem, vmem_limit_bytes=VMEM_LIMIT)


def _const_spec(shape, index_map):
    return pl.BlockSpec(shape, index_map, pipeline_mode=pl.Buffered(1))


def _ada_kernel(c_ref, w_ref, b_ref, o_ref):
    c = c_ref[...]
    c_act = (c * jax.nn.sigmoid(c)).astype(BF16)
    o_ref[...] = jnp.dot(c_act, w_ref[...].astype(BF16),
                         preferred_element_type=F32) + b_ref[...]


def _ada(c, w_ada, b_ada):
    b, d = c.shape
    n = w_ada.shape[1]
    tn = 1024
    return pl.pallas_call(
        _ada_kernel,
        grid=(n // tn,),
        in_specs=[pl.BlockSpec((b, d), lambda j: (0, 0)),
                  pl.BlockSpec((d, tn), lambda j: (0, j)),
                  pl.BlockSpec((1, tn), lambda j: (0, j))],
        out_specs=pl.BlockSpec((b, tn), lambda j: (0, j)),
        out_shape=jax.ShapeDtypeStruct((b, n), F32),
        compiler_params=_cparams(("arbitrary",)),
        name="ada",
    )(c, w_ada, b_ada.reshape(1, n))


def _rope_kernel(cos_ref, sin_ref):
    s, half = cos_ref.shape
    i = lax.broadcasted_iota(jnp.int32, (s, half), 1).astype(F32)
    pos = lax.broadcasted_iota(jnp.int32, (s, half), 0).astype(F32)
    inv_freq = jnp.power(ROPE_BASE, -(2.0 * i) / RET_QK_DIM)
    ang = pos * inv_freq
    cos_ref[...] = jnp.cos(ang)
    sin_ref[...] = jnp.sin(ang)


def _rope_tables(s_len):
    half = RET_QK_DIM // 2
    shp = jax.ShapeDtypeStruct((s_len, half), F32)
    return pl.pallas_call(
        _rope_kernel,
        out_shape=(shp, shp),
        compiler_params=pltpu.CompilerParams(vmem_limit_bytes=VMEM_LIMIT),
        name="rope",
    )()


def _inproj_kernel(x_ref, g_ref, shift_ref, scale_ref, w_ref, o_ref, h_ref):
    @pl.when(pl.program_id(2) == 0)
    def _():
        x = x_ref[0]
        y = x * lax.rsqrt(jnp.mean(x * x, axis=-1, keepdims=True) + EPS)
        y = y * g_ref[...]
        h = y * (1.0 + scale_ref[0, 0]) + shift_ref[0, 0]
        h_ref[...] = h.astype(BF16)

    o_ref[0] = jnp.dot(h_ref[...], w_ref[...], preferred_element_type=F32).astype(BF16)


def _inproj(x, g_norm1, mod4, w_in_bf16):
    b, s, d = x.shape
    n = w_in_bf16.shape[1]
    return pl.pallas_call(
        _inproj_kernel,
        grid=(b, s // TM_IN, n // TN_IN),
        in_specs=[pl.BlockSpec((1, TM_IN, d), lambda bi, si, ni: (bi, si, 0)),
                  pl.BlockSpec((1, d), lambda bi, si, ni: (0, 0)),
                  pl.BlockSpec((1, 1, 1, d), lambda bi, si, ni: (bi, 0, 0, 0)),
                  pl.BlockSpec((1, 1, 1, d), lambda bi, si, ni: (bi, 1, 0, 0)),
                  pl.BlockSpec((d, TN_IN), lambda bi, si, ni: (0, ni))],
        out_specs=pl.BlockSpec((1, TM_IN, TN_IN), lambda bi, si, ni: (bi, si, ni)),
        out_shape=jax.ShapeDtypeStruct((b, s, n), BF16),
        scratch_shapes=[pltpu.VMEM((TM_IN, d), BF16)],
        compiler_params=_cparams(("arbitrary", "arbitrary", "arbitrary")),
        name="inproj",
    )(x, g_norm1.reshape(1, d), mod4, mod4, w_in_bf16)


LOG2E = 1.4426950408889634
SIGN_BIT = 0x80000000
SB_DONE = 152.0


def _neg_abs(t):
    bits = lax.bitcast_convert_type(t, jnp.uint32) | jnp.uint32(SIGN_BIT)
    return lax.bitcast_convert_type(bits, F32)


def _sb_kernel(q_ref, k_ref, v_ref, o_ref, *, groups):
    qi = pl.program_id(2)
    tq, tk, hd = SB_TQ, SB_TK, SB_HEAD_DIM

    lane_q = lax.broadcasted_iota(jnp.int32, (tq, LANE), 1)
    ur = lax.broadcasted_iota(jnp.int32, (tk, tk), 0)
    uc = lax.broadcasted_iota(jnp.int32, (tk, tk), 1)
    u = jnp.where(ur >= uc, 1.0, 0.0).astype(BF16)
    uu = jnp.concatenate([u, u], axis=0)
    lane_v = lax.broadcasted_iota(jnp.int32, (tk, LANE), 1)
    row_t = lax.broadcasted_iota(jnp.int32, (2 * tq, tk), 0) & (tq - 1)
    col_s = lax.broadcasted_iota(jnp.int32, (2 * tq, tk), 1)
    causal = col_s < row_t

    def stacked_q(g):
        q = q_ref[0, :, g * LANE:(g + 1) * LANE] * (SB_HEAD_DIM ** -0.5)
        zq = jnp.zeros_like(q)
        return jnp.concatenate([jnp.where(lane_q < hd, q, zq),
                                jnp.where(lane_q >= hd, q, zq)], axis=0)

    qs = [stacked_q(g) for g in range(groups)]

    def scores(g, start):
        k = k_ref[0, pl.ds(start, tk), g * LANE:(g + 1) * LANE]
        return lax.dot_general(qs[g], k, (((1,), (1,)), ((), ())),
                               preferred_element_type=F32) * LOG2E

    def softplus2(z2, diag):
        sp = jnp.maximum(z2, 0.0) + jnp.log(1.0 + jnp.exp2(_neg_abs(z2))) * LOG2E
        return jnp.where(causal, sp, 0.0) if diag else sp

    def suffix_sum(sp):
        hi = sp.astype(BF16)
        lo = (sp - hi.astype(F32)).astype(BF16)
        return jnp.dot(jnp.concatenate([hi, lo], axis=1), uu,
                       preferred_element_type=F32)

    def weights(z2, carry, cum, diag):
        a = jnp.exp2(z2 - carry - cum)
        if diag:
            a = jnp.where(causal, a, 0.0)
        a = a.astype(BF16)
        return jnp.concatenate([a[:tq], a[tq:]], axis=1)

    def weighted_values(g, start, aa):
        v = v_ref[0, pl.ds(start, tk), g * LANE:(g + 1) * LANE]
        zv = jnp.zeros_like(v)
        vv = jnp.concatenate([jnp.where(lane_v < hd, v, zv),
                              jnp.where(lane_v >= hd, v, zv)], axis=0)
        return jnp.dot(aa, vv, preferred_element_type=F32)

    def tile(kb, state, diag):
        start = pl.multiple_of(kb * tk, tk)
        rng = range(groups)
        z2 = [scores(g, start) for g in rng]
        sp = [softplus2(z2[g], diag) for g in rng]
        cum = [suffix_sum(sp[g]) for g in rng]
        aa = [weights(z2[g], state[g][0], cum[g], diag) for g in rng]
        pv = [weighted_values(g, start, aa[g]) for g in rng]
        return tuple((state[g][0] + jnp.sum(sp[g], axis=-1, keepdims=True), state[g][1] + pv[g])
                     for g in rng)

    state0 = tuple((jnp.zeros((2 * tq, 1), F32), jnp.zeros((tq, LANE), F32))
                   for _ in range(groups))

    def min_carry(st):
        return functools.reduce(jnp.minimum, [jnp.min(s[0]) for s in st])

    def cond(c):
        return jnp.logical_and(c[0] < qi, c[1] < SB_DONE)

    def body(c):
        st = tile(qi - 1 - c[0], c[2], False)
        return c[0] + 1, min_carry(st), st

    state = tile(qi, state0, True)
    _, _, state = lax.while_loop(cond, body, (jnp.int32(0), min_carry(state), state))
    for g in range(groups):
        o_ref[0, :, g * LANE:(g + 1) * LANE] = state[g][1].astype(BF16)


def _sb_attention(proj):
    b, s, _ = proj.shape
    w = SB_G * LANE
    return pl.pallas_call(
        functools.partial(_sb_kernel, groups=SB_G),
        grid=(b, SB_WIDTH // w, s // SB_TQ),
        in_specs=[pl.BlockSpec((1, SB_TQ, w), lambda bi, hg, qi: (bi, qi, OFF_SB_Q // w + hg)),
                  pl.BlockSpec((1, s, w), lambda bi, hg, qi: (bi, 0, OFF_SB_K // w + hg)),
                  pl.BlockSpec((1, s, w), lambda bi, hg, qi: (bi, 0, OFF_SB_V // w + hg))],
        out_specs=pl.BlockSpec((1, SB_TQ, w), lambda bi, hg, qi: (bi, qi, hg)),
        out_shape=jax.ShapeDtypeStruct((b, s, SB_WIDTH), BF16),
        compiler_params=_cparams(("arbitrary", "arbitrary", "arbitrary")),
        name="sb",
    )(proj, proj, proj)


def _rotate(t, cos, sin):
    half = RET_QK_DIM // 2
    t1 = t[:, :half]
    t2 = t[:, half:]
    return jnp.concatenate([t1 * cos - t2 * sin, t1 * sin + t2 * cos], axis=-1)


def _ret_kernel(lg_ref, q_ref, k_ref, v_ref, g_ref, cos_ref, sin_ref, o_ref,
                state_ref, dmask_ref, qdec_ref, kdec_ref):
    h = pl.program_id(1)
    n = pl.program_id(2)
    c = RET_C
    log_gamma = lg_ref[h]

    @pl.when(n == 0)
    def _():
        state_ref[...] = jnp.zeros_like(state_ref)
        r = lax.broadcasted_iota(jnp.int32, (c, c), 0)
        cc = lax.broadcasted_iota(jnp.int32, (c, c), 1)
        diff = (r - cc).astype(F32)
        dmask_ref[...] = jnp.where(diff >= 0.0, jnp.exp(jnp.maximum(diff, 0.0) * log_gamma), 0.0)
        idx = lax.broadcasted_iota(jnp.int32, (c, 1), 0).astype(F32)
        qdec_ref[...] = jnp.exp((idx + 1.0) * log_gamma)
        kdec_ref[...] = jnp.exp((c - 1.0 - idx) * log_gamma)

    cos = cos_ref[...]
    sin = sin_ref[...]
    qr = _rotate(q_ref[0].astype(F32), cos, sin).astype(BF16)
    kr = _rotate(k_ref[0].astype(F32), cos, sin) * (RET_QK_DIM ** -0.5)
    kd = (kr * kdec_ref[...]).astype(BF16)
    kr = kr.astype(BF16)
    v = v_ref[0]

    scores = lax.dot_general(qr, kr, (((1,), (1,)), ((), ())),
                             preferred_element_type=F32) * dmask_ref[...]
    intra = jnp.dot(scores.astype(BF16), v, preferred_element_type=F32)
    state = state_ref[...]
    cross = jnp.dot(qr, state.astype(BF16), preferred_element_type=F32) * qdec_ref[...]
    chunk_decay = jnp.exp(jnp.full((1, RET_V_DIM), c * log_gamma, F32))
    state_ref[...] = state * chunk_decay + lax.dot_general(
        kd, v, (((0,), (0,)), ((), ())), preferred_element_type=F32)

    out = intra + cross
    out = out * lax.rsqrt(jnp.mean(out * out, axis=-1, keepdims=True) + EPS)
    g = g_ref[0].astype(F32)
    o_ref[0] = (g * jax.nn.sigmoid(g) * out).astype(BF16)


def _retention(proj, cos, sin, log_gamma):
    b, s, _ = proj.shape
    c = RET_C
    qk_blk = RET_QK_DIM
    v_blk = RET_V_DIM
    return pl.pallas_call(
        _ret_kernel,
        grid=(b, RET_HEADS, s // c),
        in_specs=[pl.BlockSpec(memory_space=pltpu.SMEM),
                  pl.BlockSpec((1, c, qk_blk), lambda bi, h, n: (bi, n, OFF_R_Q // qk_blk + h)),
                  pl.BlockSpec((1, c, qk_blk), lambda bi, h, n: (bi, n, OFF_R_K // qk_blk + h)),
                  pl.BlockSpec((1, c, v_blk), lambda bi, h, n: (bi, n, OFF_R_V // v_blk + h)),
                  pl.BlockSpec((1, c, v_blk), lambda bi, h, n: (bi, n, OFF_R_G // v_blk + h)),
                  pl.BlockSpec((c, RET_QK_DIM // 2), lambda bi, h, n: (n, 0)),
                  pl.BlockSpec((c, RET_QK_DIM // 2), lambda bi, h, n: (n, 0))],
        out_specs=pl.BlockSpec((1, c, v_blk), lambda bi, h, n: (bi, n, h)),
        out_shape=jax.ShapeDtypeStruct((b, s, RET_V_WIDTH), BF16),
        scratch_shapes=[pltpu.VMEM((RET_QK_DIM, RET_V_DIM), F32),
                        pltpu.VMEM((c, c), F32),
                        pltpu.VMEM((c, 1), F32),
                        pltpu.VMEM((c, 1), F32)],
        compiler_params=_cparams(("arbitrary", "arbitrary", "arbitrary")),
        name="ret",
    )(log_gamma, proj, proj, proj, proj, cos, sin)


def _merge_kernel(x_ref, osb_ref, oret_ref, brsb_ref, brret_ref, bb_ref, gate_ref,
                  wsb_ref, wret_ref, wout_ref, o_ref):
    d = D_MODEL
    p_sb = jnp.dot(osb_ref[0], wsb_ref[...], preferred_element_type=F32)
    p_ret = jnp.dot(oret_ref[0], wret_ref[...], preferred_element_type=F32)
    g_sb = jax.nn.sigmoid(brsb_ref[0].astype(F32) + bb_ref[:, :d])
    g_ret = jax.nn.sigmoid(brret_ref[0].astype(F32) + bb_ref[:, d:])
    merged = (g_sb * p_sb + g_ret * p_ret).astype(BF16)
    o_ref[0] = x_ref[0] + gate_ref[0, 0] * jnp.dot(merged, wout_ref[...],
                                                  preferred_element_type=F32)


def _merge(x, o_sb, o_ret, proj, b_branch, mod4, w_sb, w_ret, w_out):
    b, s, d = x.shape
    tm = TM_OUT
    tok = lambda bi, si: (bi, si, 0)
    zero2 = lambda bi, si: (0, 0)
    return pl.pallas_call(
        _merge_kernel,
        grid=(b, s // tm),
        in_specs=[pl.BlockSpec((1, tm, d), tok),
                  pl.BlockSpec((1, tm, SB_WIDTH), tok),
                  pl.BlockSpec((1, tm, RET_V_WIDTH), tok),
                  pl.BlockSpec((1, tm, d), lambda bi, si: (bi, si, OFF_BR // d)),
                  pl.BlockSpec((1, tm, d), lambda bi, si: (bi, si, OFF_BR // d + 1)),
                  pl.BlockSpec((1, 2 * d), zero2),
                  pl.BlockSpec((1, 1, 1, d), lambda bi, si: (bi, 2, 0, 0)),
                  _const_spec((SB_WIDTH, d), zero2),
                  _const_spec((RET_V_WIDTH, d), zero2),
                  _const_spec((d, d), zero2)],
        out_specs=pl.BlockSpec((1, tm, d), tok),
        out_shape=jax.ShapeDtypeStruct((b, s, d), F32),
        compiler_params=_cparams(("arbitrary", "arbitrary")),
        name="merge",
    )(x, o_sb, o_ret, proj, proj, b_branch.reshape(1, 2 * d), mod4, w_sb, w_ret, w_out)


def _ffn_kernel(x_ref, g2_ref, shift_ref, scale_ref, gate_ref, gf_ref, wa_ref, wu_ref, wo_ref, o_ref,
                *, final_norm):
    x = x_ref[0]
    y = x * lax.rsqrt(jnp.mean(x * x, axis=-1, keepdims=True) + EPS)
    y = y * g2_ref[...]
    h = (y * (1.0 + scale_ref[0, 0]) + shift_ref[0, 0]).astype(BF16)
    a = jnp.dot(h, wa_ref[...], preferred_element_type=F32)
    u = jnp.dot(h, wu_ref[...], preferred_element_type=F32)
    t = (a * jax.nn.sigmoid(a) * u).astype(BF16)
    x2 = x + gate_ref[0, 0] * jnp.dot(t, wo_ref[...], preferred_element_type=F32)
    if final_norm:
        r = x2 * lax.rsqrt(jnp.mean(x2 * x2, axis=-1, keepdims=True) + EPS)
        x2 = r * gf_ref[...]
    o_ref[0] = x2


def _ffn(x, g_norm2, mod4, g_final, w_a, w_u, w_o, final_norm):
    b, s, d = x.shape
    tm = TM_OUT
    tok = lambda bi, si: (bi, si, 0)
    zero2 = lambda bi, si: (0, 0)
    return pl.pallas_call(
        functools.partial(_ffn_kernel, final_norm=final_norm),
        grid=(b, s // tm),
        in_specs=[pl.BlockSpec((1, tm, d), tok),
                  pl.BlockSpec((1, d), zero2),
                  pl.BlockSpec((1, 1, 1, d), lambda bi, si: (bi, 3, 0, 0)),
                  pl.BlockSpec((1, 1, 1, d), lambda bi, si: (bi, 4, 0, 0)),
                  pl.BlockSpec((1, 1, 1, d), lambda bi, si: (bi, 5, 0, 0)),
                  pl.BlockSpec((1, d), zero2),
                  _const_spec((d, D_FF), zero2),
                  _const_spec((d, D_FF), zero2),
                  _const_spec((D_FF, d), zero2)],
        out_specs=pl.BlockSpec((1, tm, d), tok),
        out_shape=jax.ShapeDtypeStruct((b, s, d), F32),
        compiler_params=_cparams(("arbitrary", "arbitrary")),
        name="ffn",
    )(x, g_norm2.reshape(1, d), mod4, mod4, mod4, g_final.reshape(1, d), w_a, w_u, w_o)


def kernel(x, c, w_ada, b_ada, g_norm1, w_in, b_branch, w_proj_sb, w_proj_ret, w_out,
           g_norm2, w_ffn_in, w_ffn_out, g_final):
    b, s, d = x.shape
    depth = w_ada.shape[0]
    cos, sin = _rope_tables(s)
    log_gamma = jnp.log(1.0 - 2.0 ** (-5.0 - jnp.arange(RET_HEADS, dtype=jnp.float32)))
    for l in range(depth):
        mod4 = _ada(c, w_ada[l], b_ada[l]).reshape(b, N_MOD, 1, d)
        proj = _inproj(x, g_norm1[l], mod4, w_in[l].astype(BF16))
        o_sb = _sb_attention(proj)
        o_ret = _retention(proj, cos, sin, log_gamma)
        x = _merge(x, o_sb, o_ret, proj, b_branch[l], mod4,
                   w_proj_sb[l].astype(BF16), w_proj_ret[l].astype(BF16), w_out[l].astype(BF16))
        w_ffn = w_ffn_in[l].astype(BF16)
        x = _ffn(x, g_norm2[l], mod4, g_final, w_ffn[:, :D_FF], w_ffn[:, D_FF:],
                 w_ffn_out[l].astype(BF16), final_norm=(l == depth - 1))
    return x
```

```python
import functools

import jax
import jax.numpy as jnp
from jax import lax
from jax.experimental import pallas as pl
from jax.experimental.pallas import tpu as pltpu

D_MODEL = 1024
SB_HEADS = 16
SB_HEAD_DIM = 64
SB_WIDTH = SB_HEADS * SB_HEAD_DIM
RET_HEADS = 4
RET_QK_DIM = 256
RET_V_DIM = 512
RET_QK_WIDTH = RET_HEADS * RET_QK_DIM
RET_V_WIDTH = RET_HEADS * RET_V_DIM
D_FF = 2816
ROPE_BASE = 10000.0
EPS = 1e-6
N_MOD = 6

LANE = 128
OFF_SB_Q = 0
OFF_SB_K = SB_WIDTH
OFF_SB_V = 2 * SB_WIDTH
OFF_R_Q = 3 * SB_WIDTH
OFF_R_K = OFF_R_Q + RET_QK_WIDTH
OFF_R_V = OFF_R_K + RET_QK_WIDTH
OFF_R_G = OFF_R_V + RET_V_WIDTH
OFF_BR = OFF_R_G + RET_V_WIDTH
IN_WIDTH = OFF_BR + 2 * D_MODEL

F32 = jnp.float32
BF16 = jnp.bfloat16

VMEM_LIMIT = 56 * 1024 * 1024

TM_IN = 1024
TN_IN = 1024
SB_TQ = 256
SB_TK = 256
SB_G = 4
RET_C = 256
TM_OUT = 512


def _cparams(sem):
    return pltpu.CompilerParams(dimension_semantics=sem, vmem_limit_bytes=VMEM_LIMIT)


def _const_spec(shape, index_map):
    return pl.BlockSpec(shape, index_map, pipeline_mode=pl.Buffered(1))


def _ada_kernel(c_ref, w_ref, b_ref, o_ref):
    c = c_ref[...]
    c_act = (c * jax.nn.sigmoid(c)).astype(BF16)
    o_ref[...] = jnp.dot(c_act, w_ref[...].astype(BF16),
                         preferred_element_type=F32) + b_ref[...]


def _ada(c, w_ada, b_ada):
    b, d = c.shape
    n = w_ada.shape[1]
    tn = 1024
    return pl.pallas_call(
        _ada_kernel,
        grid=(n // tn,),
        in_specs=[pl.BlockSpec((b, d), lambda j: (0, 0)),
                  pl.BlockSpec((d, tn), lambda j: (0, j)),
                  pl.BlockSpec((1, tn), lambda j: (0, j))],
        out_specs=pl.BlockSpec((b, tn), lambda j: (0, j)),
        out_shape=jax.ShapeDtypeStruct((b, n), F32),
        compiler_params=_cparams(("arbitrary",)),
        name="ada",
    )(c, w_ada, b_ada.reshape(1, n))


def _rope_kernel(cos_ref, sin_ref):
    s, half = cos_ref.shape
    i = lax.broadcasted_iota(jnp.int32, (s, half), 1).astype(F32)
    pos = lax.broadcasted_iota(jnp.int32, (s, half), 0).astype(F32)
    inv_freq = jnp.power(ROPE_BASE, -(2.0 * i) / RET_QK_DIM)
    ang = pos * inv_freq
    cos_ref[...] = jnp.cos(ang)
    sin_ref[...] = jnp.sin(ang)


def _rope_tables(s_len):
    half = RET_QK_DIM // 2
    shp = jax.ShapeDtypeStruct((s_len, half), F32)
    return pl.pallas_call(
        _rope_kernel,
        out_shape=(shp, shp),
        compiler_params=pltpu.CompilerParams(vmem_limit_bytes=VMEM_LIMIT),
        name="rope",
    )()


def _rotate_heads(t, cos, sin):
    half = RET_QK_DIM // 2
    parts = []
    for h in range(t.shape[1] // RET_QK_DIM):
        t1 = t[:, h * RET_QK_DIM:h * RET_QK_DIM + half]
        t2 = t[:, h * RET_QK_DIM + half:(h + 1) * RET_QK_DIM]
        parts += [t1 * cos - t2 * sin, t1 * sin + t2 * cos]
    return jnp.concatenate(parts, axis=-1)


def _inproj_kernel(x_ref, g_ref, shift_ref, scale_ref, w_ref, cos_ref, sin_ref, o_ref, h_ref):
    ni = pl.program_id(2)

    @pl.when(ni == 0)
    def _():
        x = x_ref[0]
        y = x * lax.rsqrt(jnp.mean(x * x, axis=-1, keepdims=True) + EPS)
        y = y * g_ref[...]
        h = y * (1.0 + scale_ref[0, 0]) + shift_ref[0, 0]
        h_ref[...] = h.astype(BF16)

    def proj():
        return jnp.dot(h_ref[...], w_ref[...], preferred_element_type=F32)

    is_q = ni == OFF_R_Q // TN_IN
    is_k = ni == OFF_R_K // TN_IN
    is_g = jnp.logical_and(ni >= OFF_R_G // TN_IN, ni < OFF_BR // TN_IN)

    @pl.when(is_q)
    def _():
        o_ref[0] = _rotate_heads(proj(), cos_ref[...], sin_ref[...]).astype(BF16)

    @pl.when(is_k)
    def _():
        kr = _rotate_heads(proj(), cos_ref[...], sin_ref[...]) * (RET_QK_DIM ** -0.5)
        o_ref[0] = kr.astype(BF16)

    @pl.when(is_g)
    def _():
        a = proj()
        o_ref[0] = (a * jax.nn.sigmoid(a)).astype(BF16)

    @pl.when(jnp.logical_not(is_q | is_k | is_g))
    def _():
        o_ref[0] = proj().astype(BF16)


def _inproj(x, g_norm1, mod4, w_in_bf16, cos, sin):
    b, s, d = x.shape
    n = w_in_bf16.shape[1]
    half = RET_QK_DIM // 2
    assert OFF_R_Q % TN_IN == 0 and RET_QK_WIDTH == TN_IN and OFF_R_G % TN_IN == 0 and OFF_BR % TN_IN == 0
    return pl.pallas_call(
        _inproj_kernel,
        grid=(b, s // TM_IN, n // TN_IN),
        in_specs=[pl.BlockSpec((1, TM_IN, d), lambda bi, si, ni: (bi, si, 0)),
                  pl.BlockSpec((1, d), lambda bi, si, ni: (0, 0)),
                  pl.BlockSpec((1, 1, 1, d), lambda bi, si, ni: (bi, 0, 0, 0)),
                  pl.BlockSpec((1, 1, 1, d), lambda bi, si, ni: (bi, 1, 0, 0)),
                  pl.BlockSpec((d, TN_IN), lambda bi, si, ni: (0, ni)),
                  pl.BlockSpec((TM_IN, half), lambda bi, si, ni: (si, 0)),
                  pl.BlockSpec((TM_IN, half), lambda bi, si, ni: (si, 0))],
        out_specs=pl.BlockSpec((1, TM_IN, TN_IN), lambda bi, si, ni: (bi, si, ni)),
        out_shape=jax.ShapeDtypeStruct((b, s, n), BF16),
        scratch_shapes=[pltpu.VMEM((TM_IN, d), BF16)],
        compiler_params=_cparams(("arbitrary", "arbitrary", "arbitrary")),
        name="inproj",
    )(x, g_norm1.reshape(1, d), mod4, mod4, w_in_bf16, cos, sin)


LOG2E = 1.4426950408889634
SB_DONE = 152.0


def _sb_kernel(q_ref, k_ref, v_ref, o_ref, *, groups):
    qi = pl.program_id(2)
    tq, tk, hd = SB_TQ, SB_TK, SB_HEAD_DIM

    lane_q = lax.broadcasted_iota(jnp.int32, (tq, LANE), 1)
    ur = lax.broadcasted_iota(jnp.int32, (tk, tk), 0)
    uc = lax.broadcasted_iota(jnp.int32, (tk, tk), 1)
    u = jnp.where(ur >= uc, 1.0, 0.0).astype(BF16)
    uu = jnp.concatenate([u, u], axis=0)
    lane_v = lax.broadcasted_iota(jnp.int32, (tk, LANE), 1)
    row_t = lax.broadcasted_iota(jnp.int32, (2 * tq, tk), 0) & (tq - 1)
    col_s = lax.broadcasted_iota(jnp.int32, (2 * tq, tk), 1)
    causal = col_s < row_t

    def stacked_q(g):
        q = q_ref[0, :, g * LANE:(g + 1) * LANE] * (SB_HEAD_DIM ** -0.5)
        zq = jnp.zeros_like(q)
        return jnp.concatenate([jnp.where(lane_q < hd, q, zq),
                                jnp.where(lane_q >= hd, q, zq)], axis=0)

    qs = [stacked_q(g) for g in range(groups)]

    def scores(g, start):
        k = k_ref[0, pl.ds(start, tk), g * LANE:(g + 1) * LANE]
        return lax.dot_general(qs[g], k, (((1,), (1,)), ((), ())),
                               preferred_element_type=F32) * LOG2E

    def softplus2(z2, diag):
        sp = jnp.maximum(z2, 0.0) + jnp.log(1.0 + jnp.exp2(-jnp.abs(z2))) * LOG2E
        return jnp.where(causal, sp, 0.0) if diag else sp

    def suffix_sum(sp):
        hi = sp.astype(BF16)
        lo = (sp - hi.astype(F32)).astype(BF16)
        return jnp.dot(jnp.concatenate([hi, lo], axis=1), uu,
                       preferred_element_type=F32)

    def weights(z2, carry, cum, diag):
        a = jnp.exp2(z2 - carry - cum)
        if diag:
            a = jnp.where(causal, a, 0.0)
        a = a.astype(BF16)
        return jnp.concatenate([a[:tq], a[tq:]], axis=1)

    def weighted_values(g, start, aa):
        v = v_ref[0, pl.ds(start, tk), g * LANE:(g + 1) * LANE]
        zv = jnp.zeros_like(v)
        vv = jnp.concatenate([jnp.where(lane_v < hd, v, zv),
                              jnp.where(lane_v >= hd, v, zv)], axis=0)
        return jnp.dot(aa, vv, preferred_element_type=F32)

    def tile(kb, state, diag):
        start = pl.multiple_of(kb * tk, tk)
        rng = range(groups)
        z2 = [scores(g, start) for g in rng]
        sp = [softplus2(z2[g], diag) for g in rng]
        cum = [suffix_sum(sp[g]) for g in rng]
        aa = [weights(z2[g], state[g][0], cum[g], diag) for g in rng]
        pv = [weighted_values(g, start, aa[g]) for g in rng]
        return tuple((state[g][0] + jnp.sum(sp[g], axis=-1, keepdims=True), state[g][1] + pv[g])
                     for g in rng)

    state0 = tuple((jnp.zeros((2 * tq, 1), F32), jnp.zeros((tq, LANE), F32))
                   for _ in range(groups))

    def min_carry(st):
        return functools.reduce(jnp.minimum, [jnp.min(s[0]) for s in st])

    def cond(c):
        return jnp.logical_and(c[0] < qi, c[1] < SB_DONE)

    def body(c):
        st = tile(qi - 1 - c[0], c[2], False)
        return c[0] + 1, min_carry(st), st

    state = tile(qi, state0, True)
    _, _, state = lax.while_loop(cond, body, (jnp.int32(0), min_carry(state), state))
    for g in range(groups):
        o_ref[0, :, g * LANE:(g + 1) * LANE] = state[g][1].astype(BF16)


def _sb_attention(proj):
    b, s, _ = proj.shape
    w = SB_G * LANE
    return pl.pallas_call(
        functools.partial(_sb_kernel, groups=SB_G),
        grid=(b, SB_WIDTH // w, s // SB_TQ),
        in_specs=[pl.BlockSpec((1, SB_TQ, w), lambda bi, hg, qi: (bi, qi, OFF_SB_Q // w + hg)),
                  pl.BlockSpec((1, s, w), lambda bi, hg, qi: (bi, 0, OFF_SB_K // w + hg)),
                  pl.BlockSpec((1, s, w), lambda bi, hg, qi: (bi, 0, OFF_SB_V // w + hg))],
        out_specs=pl.BlockSpec((1, SB_TQ, w), lambda bi, hg, qi: (bi, qi, hg)),
        out_shape=jax.ShapeDtypeStruct((b, s, SB_WIDTH), BF16),
        compiler_params=_cparams(("arbitrary", "arbitrary", "arbitrary")),
        name="sb",
    )(proj, proj, proj)


def _ret_kernel(lg_ref, q_ref, k_ref, v01_ref, v23_ref, sg01_ref, sg23_ref, o_ref,
                state_ref, dmask_ref, qdec_ref, kdec_ref):
    n = pl.program_id(1)
    c = RET_C
    dk, dv = RET_QK_DIM, RET_V_DIM
    v_refs = (v01_ref, v23_ref)
    sg_refs = (sg01_ref, sg23_ref)

    @pl.when(n == 0)
    def _():
        state_ref[...] = jnp.zeros_like(state_ref)
        r = lax.broadcasted_iota(jnp.int32, (c, c), 0)
        cc = lax.broadcasted_iota(jnp.int32, (c, c), 1)
        diff = (r - cc).astype(F32)
        idx = lax.broadcasted_iota(jnp.int32, (c, 1), 0).astype(F32)
        for h in range(RET_HEADS):
            lg = lg_ref[h]
            dmask_ref[h] = jnp.where(diff >= 0.0, jnp.exp(jnp.maximum(diff, 0.0) * lg), 0.0)
            qdec_ref[h] = jnp.exp((idx + 1.0) * lg)
            kdec_ref[h] = jnp.exp((c - 1.0 - idx) * lg)

    heads = range(RET_HEADS)
    vcol = [slice((h % 2) * dv, (h % 2 + 1) * dv) for h in heads]
    q = [q_ref[0, :, h * dk:(h + 1) * dk] for h in heads]
    k = [k_ref[0, :, h * dk:(h + 1) * dk] for h in heads]
    v = [v_refs[h // 2][0, :, vcol[h]] for h in heads]
    scores = [lax.dot_general(q[h], k[h], (((1,), (1,)), ((), ())),
                              preferred_element_type=F32) for h in heads]
    state = [state_ref[h] for h in heads]
    cross = [jnp.dot(q[h], state[h].astype(BF16), preferred_element_type=F32) for h in heads]
    intra = [jnp.dot((scores[h] * dmask_ref[h]).astype(BF16), v[h],
                     preferred_element_type=F32) for h in heads]
    kd = [(k[h].astype(F32) * kdec_ref[h]).astype(BF16) for h in heads]
    upd = [lax.dot_general(kd[h], v[h], (((0,), (0,)), ((), ())),
                           preferred_element_type=F32) for h in heads]
    for h in heads:
        chunk_decay = jnp.exp(jnp.full((1, dv), c * lg_ref[h], F32))
        state_ref[h] = state[h] * chunk_decay + upd[h]
        out = intra[h] + cross[h] * qdec_ref[h]
        out = out * lax.rsqrt(jnp.mean(out * out, axis=-1, keepdims=True) + EPS)
        sg = sg_refs[h // 2][0, :, vcol[h]].astype(F32)
        o_ref[0, :, h * dv:(h + 1) * dv] = (sg * out).astype(BF16)


def _retention(proj, log_gamma):
    b, s, _ = proj.shape
    c = RET_C
    qw, vw = RET_QK_WIDTH, RET_V_WIDTH
    hw = 2 * RET_V_DIM
    return pl.pallas_call(
        _ret_kernel,
        grid=(b, s // c),
        in_specs=[pl.BlockSpec(memory_space=pltpu.SMEM),
                  pl.BlockSpec((1, c, qw), lambda bi, n: (bi, n, OFF_R_Q // qw)),
                  pl.BlockSpec((1, c, qw), lambda bi, n: (bi, n, OFF_R_K // qw)),
                  pl.BlockSpec((1, c, hw), lambda bi, n: (bi, n, OFF_R_V // hw)),
                  pl.BlockSpec((1, c, hw), lambda bi, n: (bi, n, OFF_R_V // hw + 1)),
                  pl.BlockSpec((1, c, hw), lambda bi, n: (bi, n, OFF_R_G // hw)),
                  pl.BlockSpec((1, c, hw), lambda bi, n: (bi, n, OFF_R_G // hw + 1))],
        out_specs=pl.BlockSpec((1, c, vw), lambda bi, n: (bi, n, 0)),
        out_shape=jax.ShapeDtypeStruct((b, s, vw), BF16),
        scratch_shapes=[pltpu.VMEM((RET_HEADS, RET_QK_DIM, RET_V_DIM), F32),
                        pltpu.VMEM((RET_HEADS, c, c), F32),
                        pltpu.VMEM((RET_HEADS, c, 1), F32),
                        pltpu.VMEM((RET_HEADS, c, 1), F32)],
        compiler_params=_cparams(("arbitrary", "arbitrary")),
        name="ret",
    )(log_gamma, proj, proj, proj, proj, proj, proj)


def _merge_kernel(x_ref, osb_ref, oret_ref, brsb_ref, brret_ref, bb_ref, gate_ref,
                  wsb_ref, wret_ref, wout_ref, o_ref):
    d = D_MODEL
    p_sb = jnp.dot(osb_ref[0], wsb_ref[...], preferred_element_type=F32)
    p_ret = jnp.dot(oret_ref[0], wret_ref[...], preferred_element_type=F32)
    g_sb = jax.nn.sigmoid(brsb_ref[0].astype(F32) + bb_ref[:, :d])
    g_ret = jax.nn.sigmoid(brret_ref[0].astype(F32) + bb_ref[:, d:])
    merged = (g_sb * p_sb + g_ret * p_ret).astype(BF16)
    o_ref[0] = x_ref[0] + gate_ref[0, 0] * jnp.dot(merged, wout_ref[...],
                                                  preferred_element_type=F32)


def _merge(x, o_sb, o_ret, proj, b_branch, mod4, w_sb, w_ret, w_out):
    b, s, d = x.shape
    tm = TM_OUT
    tok = lambda bi, si: (bi, si, 0)
    zero2 = lambda bi, si: (0, 0)
    return pl.pallas_call(
        _merge_kernel,
        grid=(b, s // tm),
        in_specs=[pl.BlockSpec((1, tm, d), tok),
                  pl.BlockSpec((1, tm, SB_WIDTH), tok),
                  pl.BlockSpec((1, tm, RET_V_WIDTH), tok),
                  pl.BlockSpec((1, tm, d), lambda bi, si: (bi, si, OFF_BR // d)),
                  pl.BlockSpec((1, tm, d), lambda bi, si: (bi, si, OFF_BR // d + 1)),
                  pl.BlockSpec((1, 2 * d), zero2),
                  pl.BlockSpec((1, 1, 1, d), lambda bi, si: (bi, 2, 0, 0)),
                  _const_spec((SB_WIDTH, d), zero2),
                  _const_spec((RET_V_WIDTH, d), zero2),
                  _const_spec((d, d), zero2)],
        out_specs=pl.BlockSpec((1, tm, d), tok),
        out_shape=jax.ShapeDtypeStruct((b, s, d), F32),
        compiler_params=_cparams(("arbitrary", "arbitrary")),
        name="merge",
    )(x, o_sb, o_ret, proj, proj, b_branch.reshape(1, 2 * d), mod4, w_sb, w_ret, w_out)


def _ffn_kernel(x_ref, g2_ref, shift_ref, scale_ref, gate_ref, gf_ref, wa_ref, wu_ref, wo_ref, o_ref,
                *, final_norm):
    x = x_ref[0]
    y = x * lax.rsqrt(jnp.mean(x * x, axis=-1, keepdims=True) + EPS)
    y = y * g2_ref[...]
    h = (y * (1.0 + scale_ref[0, 0]) + shift_ref[0, 0]).astype(BF16)
    a = jnp.dot(h, wa_ref[...], preferred_element_type=F32)
    u = jnp.dot(h, wu_ref[...], preferred_element_type=F32)
    t = (a * jax.nn.sigmoid(a) * u).astype(BF16)
    x2 = x + gate_ref[0, 0] * jnp.dot(t, wo_ref[...], preferred_element_type=F32)
    if final_norm:
        r = x2 * lax.rsqrt(jnp.mean(x2 * x2, axis=-1, keepdims=True) + EPS)
        x2 = r * gf_ref[...]
    o_ref[0] = x2


def _ffn(x, g_norm2, mod4, g_final, w_a, w_u, w_o, final_norm):
    b, s, d = x.shape
    tm = TM_OUT
    tok = lambda bi, si: (bi, si, 0)
    zero2 = lambda bi, si: (0, 0)
    return pl.pallas_call(
        functools.partial(_ffn_kernel, final_norm=final_norm),
        grid=(b, s // tm),
        in_specs=[pl.BlockSpec((1, tm, d), tok),
                  pl.BlockSpec((1, d), zero2),
                  pl.BlockSpec((1, 1, 1, d), lambda bi, si: (bi, 3, 0, 0)),
                  pl.BlockSpec((1, 1, 1, d), lambda bi, si: (bi, 4, 0, 0)),
                  pl.BlockSpec((1, 1, 1, d), lambda bi, si: (bi, 5, 0, 0)),
                  pl.BlockSpec((1, d), zero2),
                  _const_spec((d, D_FF), zero2),
                  _const_spec((d, D_FF), zero2),
                  _const_spec((D_FF, d), zero2)],
        out_specs=pl.BlockSpec((1, tm, d), tok),
        out_shape=jax.ShapeDtypeStruct((b, s, d), F32),
        compiler_params=_cparams(("arbitrary", "arbitrary")),
        name="ffn",
    )(x, g_norm2.reshape(1, d), mod4, mod4, mod4, g_final.reshape(1, d), w_a, w_u, w_o)


def kernel(x, c, w_ada, b_ada, g_norm1, w_in, b_branch, w_proj_sb, w_proj_ret, w_out,
           g_norm2, w_ffn_in, w_ffn_out, g_final):
    b, s, d = x.shape
    depth = w_ada.shape[0]
    cos, sin = _rope_tables(s)
    log_gamma = jnp.log(1.0 - 2.0 ** (-5.0 - jnp.arange(RET_HEADS, dtype=jnp.float32)))
    for l in range(depth):
        mod4 = _ada(c, w_ada[l], b_ada[l]).reshape(b, N_MOD, 1, d)
        proj = _inproj(x, g_norm1[l], mod4, w_in[l].astype(BF16), cos, sin)
        o_sb = _sb_attention(proj)
        o_ret = _retention(proj, log_gamma)
        x = _merge(x, o_sb, o_ret, proj, b_branch[l], mod4,
                   w_proj_sb[l].astype(BF16), w_proj_ret[l].astype(BF16), w_out[l].astype(BF16))
        w_ffn = w_ffn_in[l].astype(BF16)
        x = _ffn(x, g_norm2[l], mod4, g_final, w_ffn[:, :D_FF], w_ffn[:, D_FF:],
                 w_ffn_out[l].astype(BF16), final_norm=(l == depth - 1))
    return x
```

```python
import functools

import jax
import jax.numpy as jnp
from jax import lax
from jax.experimental import pallas as pl
from jax.experimental.pallas import tpu as pltpu

D_MODEL = 1024
SB_HEADS = 16
SB_HEAD_DIM = 64
SB_WIDTH = SB_HEADS * SB_HEAD_DIM
RET_HEADS = 4
RET_QK_DIM = 256
RET_V_DIM = 512
RET_QK_WIDTH = RET_HEADS * RET_QK_DIM
RET_V_WIDTH = RET_HEADS * RET_V_DIM
D_FF = 2816
ROPE_BASE = 10000.0
EPS = 1e-6
N_MOD = 6
LOG2E = 1.4426950408889634

LANE = 128
OFF_SB_Q = 0
OFF_SB_K = SB_WIDTH
OFF_SB_V = 2 * SB_WIDTH
OFF_R_Q = 3 * SB_WIDTH
OFF_R_K = OFF_R_Q + RET_QK_WIDTH
OFF_R_V = OFF_R_K + RET_QK_WIDTH
OFF_R_G = OFF_R_V + RET_V_WIDTH
OFF_BR = OFF_R_G + RET_V_WIDTH
IN_WIDTH = OFF_BR + 2 * D_MODEL

F32 = jnp.float32
BF16 = jnp.bfloat16

VMEM_LIMIT = 56 * 1024 * 1024

TM_IN = 2048
TN_IN = 1024
SB_TQ = 256
SB_TK = 256
SB_G = 4
RET_C = 256
TM_OUT = 512


def _cparams(sem):
    return pltpu.CompilerParams(dimension_semantics=sem, vmem_limit_bytes=VMEM_LIMIT)


def _const_spec(shape, index_map):
    return pl.BlockSpec(shape, index_map, pipeline_mode=pl.Buffered(1))


def _ada_kernel(c_ref, w_ref, b_ref, o_ref):
    c = c_ref[...]
    c_act = (c * jax.nn.sigmoid(c)).astype(BF16)
    o_ref[...] = jnp.dot(c_act, w_ref[...].astype(BF16),
                         preferred_element_type=F32) + b_ref[...]


def _ada(c, w_ada, b_ada):
    b, d = c.shape
    n = w_ada.shape[1]
    tn = 1024
    return pl.pallas_call(
        _ada_kernel,
        grid=(n // tn,),
        in_specs=[pl.BlockSpec((b, d), lambda j: (0, 0)),
                  pl.BlockSpec((d, tn), lambda j: (0, j)),
                  pl.BlockSpec((1, tn), lambda j: (0, j))],
        out_specs=pl.BlockSpec((b, tn), lambda j: (0, j)),
        out_shape=jax.ShapeDtypeStruct((b, n), F32),
        compiler_params=_cparams(("arbitrary",)),
        name="ada",
    )(c, w_ada, b_ada.reshape(1, n))


def _rope_kernel(cos_ref, sin_ref):
    s, half = cos_ref.shape
    i = lax.broadcasted_iota(jnp.int32, (s, half), 1).astype(F32)
    pos = lax.broadcasted_iota(jnp.int32, (s, half), 0).astype(F32)
    inv_freq = jnp.power(ROPE_BASE, -(2.0 * i) / RET_QK_DIM)
    ang = pos * inv_freq
    cos_ref[...] = jnp.cos(ang)
    sin_ref[...] = jnp.sin(ang)


def _rope_tables(s_len):
    half = RET_QK_DIM // 2
    shp = jax.ShapeDtypeStruct((s_len, half), F32)
    return pl.pallas_call(
        _rope_kernel,
        out_shape=(shp, shp),
        compiler_params=pltpu.CompilerParams(vmem_limit_bytes=VMEM_LIMIT),
        name="rope",
    )()


def _rotate_heads(t, cos, sin):
    half = RET_QK_DIM // 2
    parts = []
    for h in range(t.shape[1] // RET_QK_DIM):
        t1 = t[:, h * RET_QK_DIM:h * RET_QK_DIM + half]
        t2 = t[:, h * RET_QK_DIM + half:(h + 1) * RET_QK_DIM]
        parts += [t1 * cos - t2 * sin, t1 * sin + t2 * cos]
    return jnp.concatenate(parts, axis=-1)


def _inproj_kernel(x_ref, g_ref, shift_ref, scale_ref, w_ref, cos_ref, sin_ref, o_ref, h_ref):
    ni = pl.program_id(2)

    @pl.when(ni == 0)
    def _():
        x = x_ref[0]
        y = x * lax.rsqrt(jnp.mean(x * x, axis=-1, keepdims=True) + EPS)
        y = y * g_ref[...]
        h = y * (1.0 + scale_ref[0, 0]) + shift_ref[0, 0]
        h_ref[...] = h.astype(BF16)

    def proj():
        return jnp.dot(h_ref[...], w_ref[...], preferred_element_type=F32)

    is_sbq = ni == OFF_SB_Q // TN_IN
    is_q = ni == OFF_R_Q // TN_IN
    is_k = ni == OFF_R_K // TN_IN
    is_g = jnp.logical_and(ni >= OFF_R_G // TN_IN, ni < OFF_BR // TN_IN)

    @pl.when(is_sbq)
    def _():
        o_ref[0] = (proj() * (SB_HEAD_DIM ** -0.5 * LOG2E)).astype(BF16)

    @pl.when(is_q)
    def _():
        o_ref[0] = _rotate_heads(proj(), cos_ref[...], sin_ref[...]).astype(BF16)

    @pl.when(is_k)
    def _():
        kr = _rotate_heads(proj(), cos_ref[...], sin_ref[...]) * (RET_QK_DIM ** -0.5)
        o_ref[0] = kr.astype(BF16)

    @pl.when(is_g)
    def _():
        a = proj()
        o_ref[0] = (a * jax.nn.sigmoid(a)).astype(BF16)

    @pl.when(jnp.logical_not(is_sbq | is_q | is_k | is_g))
    def _():
        o_ref[0] = proj().astype(BF16)


def _inproj(x, g_norm1, mod4, w_in_bf16, cos, sin):
    b, s, d = x.shape
    n = w_in_bf16.shape[1]
    half = RET_QK_DIM // 2
    assert OFF_R_Q % TN_IN == 0 and RET_QK_WIDTH == TN_IN and OFF_R_G % TN_IN == 0 and OFF_BR % TN_IN == 0
    assert SB_WIDTH == TN_IN
    return pl.pallas_call(
        _inproj_kernel,
        grid=(b, s // TM_IN, n // TN_IN),
        in_specs=[pl.BlockSpec((1, TM_IN, d), lambda bi, si, ni: (bi, si, 0)),
                  pl.BlockSpec((1, d), lambda bi, si, ni: (0, 0)),
                  pl.BlockSpec((1, 1, 1, d), lambda bi, si, ni: (bi, 0, 0, 0)),
                  pl.BlockSpec((1, 1, 1, d), lambda bi, si, ni: (bi, 1, 0, 0)),
                  pl.BlockSpec((d, TN_IN), lambda bi, si, ni: (0, ni)),
                  pl.BlockSpec((TM_IN, half), lambda bi, si, ni: (si, 0)),
                  pl.BlockSpec((TM_IN, half), lambda bi, si, ni: (si, 0))],
        out_specs=pl.BlockSpec((1, TM_IN, TN_IN), lambda bi, si, ni: (bi, si, ni)),
        out_shape=jax.ShapeDtypeStruct((b, s, n), BF16),
        scratch_shapes=[pltpu.VMEM((TM_IN, d), BF16)],
        compiler_params=_cparams(("arbitrary", "arbitrary", "arbitrary")),
        name="inproj",
    )(x, g_norm1.reshape(1, d), mod4, mod4, w_in_bf16, cos, sin)


SB_DONE = 152.0


def _sb_kernel(q_ref, k_ref, v_ref, o_ref, *, groups):
    qi = pl.program_id(2)
    tq, tk, hd = SB_TQ, SB_TK, SB_HEAD_DIM

    lane_q = lax.broadcasted_iota(jnp.int32, (tq, LANE), 1)
    ur = lax.broadcasted_iota(jnp.int32, (tk, tk), 0)
    uc = lax.broadcasted_iota(jnp.int32, (tk, tk), 1)
    u = jnp.where(ur > uc, 1.0, 0.0).astype(BF16)
    lane_v = lax.broadcasted_iota(jnp.int32, (tk, LANE), 1)
    row_t = lax.broadcasted_iota(jnp.int32, (2 * tq, tk), 0) & (tq - 1)
    col_s = lax.broadcasted_iota(jnp.int32, (2 * tq, tk), 1)
    causal = col_s < row_t

    def stacked_q(g):
        q = q_ref[0, :, g * LANE:(g + 1) * LANE]
        zq = jnp.zeros_like(q)
        return jnp.concatenate([jnp.where(lane_q < hd, q, zq),
                                jnp.where(lane_q >= hd, q, zq)], axis=0)

    qs = [stacked_q(g) for g in range(groups)]

    def scores(g, start):
        k = k_ref[0, pl.ds(start, tk), g * LANE:(g + 1) * LANE]
        return lax.dot_general(qs[g], k, (((1,), (1,)), ((), ())),
                               preferred_element_type=F32)

    def softplus2(z2, diag):
        sp = jnp.maximum(z2, 0.0) + jnp.log(1.0 + jnp.exp2(-jnp.abs(z2))) * LOG2E
        return (jnp.where(causal, sp, 0.0) if diag else sp), z2 - sp

    def suffix_sum(sp):
        return jnp.dot(sp.astype(BF16), u, preferred_element_type=F32)

    def weights(ls, carry, cum, diag):
        a = jnp.exp2(ls - carry - cum)
        if diag:
            a = jnp.where(causal, a, 0.0)
        a = a.astype(BF16)
        return jnp.concatenate([a[:tq], a[tq:]], axis=1)

    def weighted_values(g, start, aa):
        v = v_ref[0, pl.ds(start, tk), g * LANE:(g + 1) * LANE]
        zv = jnp.zeros_like(v)
        vv = jnp.concatenate([jnp.where(lane_v < hd, v, zv),
                              jnp.where(lane_v >= hd, v, zv)], axis=0)
        return jnp.dot(aa, vv, preferred_element_type=F32)

    def tile(kb, state, diag):
        start = pl.multiple_of(kb * tk, tk)
        rng = range(groups)
        z2 = [scores(g, start) for g in rng]
        sp, ls = zip(*[softplus2(z2[g], diag) for g in rng])
        cum = [suffix_sum(sp[g]) for g in rng]
        aa = [weights(ls[g], state[g][0], cum[g], diag) for g in rng]
        pv = [weighted_values(g, start, aa[g]) for g in rng]
        return tuple((state[g][0] + jnp.sum(sp[g], axis=-1, keepdims=True), state[g][1] + pv[g])
                     for g in rng)

    state0 = tuple((jnp.zeros((2 * tq, 1), F32), jnp.zeros((tq, LANE), F32))
                   for _ in range(groups))

    def min_carry(st):
        return functools.reduce(jnp.minimum, [jnp.min(s[0]) for s in st])

    def cond(c):
        return jnp.logical_and(c[0] < qi, c[1] < SB_DONE)

    def body(c):
        st = tile(qi - 1 - c[0], c[2], False)
        return c[0] + 1, min_carry(st), st

    state = tile(qi, state0, True)
    _, _, state = lax.while_loop(cond, body, (jnp.int32(0), min_carry(state), state))
    for g in range(groups):
        o_ref[0, :, g * LANE:(g + 1) * LANE] = state[g][1].astype(BF16)


def _sb_attention(proj):
    b, s, _ = proj.shape
    w = SB_G * LANE
    return pl.pallas_call(
        functools.partial(_sb_kernel, groups=SB_G),
        grid=(b, SB_WIDTH // w, s // SB_TQ),
        in_specs=[pl.BlockSpec((1, SB_TQ, w), lambda bi, hg, qi: (bi, qi, OFF_SB_Q // w + hg)),
                  pl.BlockSpec((1, s, w), lambda bi, hg, qi: (bi, 0, OFF_SB_K // w + hg)),
                  pl.BlockSpec((1, s, w), lambda bi, hg, qi: (bi, 0, OFF_SB_V // w + hg))],
        out_specs=pl.BlockSpec((1, SB_TQ, w), lambda bi, hg, qi: (bi, qi, hg)),
        out_shape=jax.ShapeDtypeStruct((b, s, SB_WIDTH), BF16),
        compiler_params=_cparams(("arbitrary", "arbitrary", "arbitrary")),
        name="sb",
    )(proj, proj, proj)


def _ret_kernel(lg_ref, q_ref, k_ref, v01_ref, v23_ref, sg01_ref, sg23_ref, o_ref,
                state_ref, dmask_ref, qdec_ref, kdec_ref):
    n = pl.program_id(1)
    c = RET_C
    dk, dv = RET_QK_DIM, RET_V_DIM
    v_refs = (v01_ref, v23_ref)
    sg_refs = (sg01_ref, sg23_ref)

    @pl.when(n == 0)
    def _():
        state_ref[...] = jnp.zeros_like(state_ref)
        r = lax.broadcasted_iota(jnp.int32, (c, c), 0)
        cc = lax.broadcasted_iota(jnp.int32, (c, c), 1)
        diff = (r - cc).astype(F32)
        idx = lax.broadcasted_iota(jnp.int32, (c, 1), 0).astype(F32)
        for h in range(RET_HEADS):
            lg = lg_ref[h]
            dmask_ref[h] = jnp.where(diff >= 0.0, jnp.exp(jnp.maximum(diff, 0.0) * lg), 0.0)
            qdec_ref[h] = jnp.exp((idx + 1.0) * lg)
            kdec_ref[h] = jnp.exp((c - 1.0 - idx) * lg)

    heads = range(RET_HEADS)
    vcol = [slice((h % 2) * dv, (h % 2 + 1) * dv) for h in heads]
    q = [q_ref[0, :, h * dk:(h + 1) * dk] for h in heads]
    k = [k_ref[0, :, h * dk:(h + 1) * dk] for h in heads]
    v = [v_refs[h // 2][0, :, vcol[h]] for h in heads]
    scores = [lax.dot_general(q[h], k[h], (((1,), (1,)), ((), ())),
                              preferred_element_type=F32) for h in heads]
    state = [state_ref[h] for h in heads]
    cross = [jnp.dot(q[h], state[h].astype(BF16), preferred_element_type=F32) for h in heads]
    intra = [jnp.dot((scores[h] * dmask_ref[h]).astype(BF16), v[h],
                     preferred_element_type=F32) for h in heads]
    kd = [(k[h].astype(F32) * kdec_ref[h]).astype(BF16) for h in heads]
    upd = [lax.dot_general(kd[h], v[h], (((0,), (0,)), ((), ())),
                           preferred_element_type=F32) for h in heads]
    for h in heads:
        chunk_decay = jnp.exp(jnp.full((1, dv), c * lg_ref[h], F32))
        state_ref[h] = state[h] * chunk_decay + upd[h]
        out = intra[h] + cross[h] * qdec_ref[h]
        out = out * lax.rsqrt(jnp.mean(out * out, axis=-1, keepdims=True) + EPS)
        sg = sg_refs[h // 2][0, :, vcol[h]].astype(F32)
        o_ref[0, :, h * dv:(h + 1) * dv] = (sg * out).astype(BF16)


def _retention(proj, log_gamma):
    b, s, _ = proj.shape
    c = RET_C
    qw, vw = RET_QK_WIDTH, RET_V_WIDTH
    hw = 2 * RET_V_DIM
    return pl.pallas_call(
        _ret_kernel,
        grid=(b, s // c),
        in_specs=[pl.BlockSpec(memory_space=pltpu.SMEM),
                  pl.BlockSpec((1, c, qw), lambda bi, n: (bi, n, OFF_R_Q // qw)),
                  pl.BlockSpec((1, c, qw), lambda bi, n: (bi, n, OFF_R_K // qw)),
                  pl.BlockSpec((1, c, hw), lambda bi, n: (bi, n, OFF_R_V // hw)),
                  pl.BlockSpec((1, c, hw), lambda bi, n: (bi, n, OFF_R_V // hw + 1)),
                  pl.BlockSpec((1, c, hw), lambda bi, n: (bi, n, OFF_R_G // hw)),
                  pl.BlockSpec((1, c, hw), lambda bi, n: (bi, n, OFF_R_G // hw + 1))],
        out_specs=pl.BlockSpec((1, c, vw), lambda bi, n: (bi, n, 0)),
        out_shape=jax.ShapeDtypeStruct((b, s, vw), BF16),
        scratch_shapes=[pltpu.VMEM((RET_HEADS, RET_QK_DIM, RET_V_DIM), F32),
                        pltpu.VMEM((RET_HEADS, c, c), F32),
                        pltpu.VMEM((RET_HEADS, c, 1), F32),
                        pltpu.VMEM((RET_HEADS, c, 1), F32)],
        compiler_params=_cparams(("arbitrary", "arbitrary")),
        name="ret",
    )(log_gamma, proj, proj, proj, proj, proj, proj)


def _merge_kernel(x_ref, osb_ref, oret_ref, brsb_ref, brret_ref, bb_ref, gate_ref,
                  wsb_ref, wret_ref, wout_ref, o_ref):
    d = D_MODEL
    p_sb = jnp.dot(osb_ref[0], wsb_ref[...], preferred_element_type=F32)
    p_ret = jnp.dot(oret_ref[0], wret_ref[...], preferred_element_type=F32)
    g_sb = jax.nn.sigmoid(brsb_ref[0].astype(F32) + bb_ref[:, :d])
    g_ret = jax.nn.sigmoid(brret_ref[0].astype(F32) + bb_ref[:, d:])
    merged = (g_sb * p_sb + g_ret * p_ret).astype(BF16)
    o_ref[0] = x_ref[0] + gate_ref[0, 0] * jnp.dot(merged, wout_ref[...],
                                                  preferred_element_type=F32)


def _merge(x, o_sb, o_ret, proj, b_branch, mod4, w_sb, w_ret, w_out):
    b, s, d = x.shape
    tm = TM_OUT
    tok = lambda bi, si: (bi, si, 0)
    zero2 = lambda bi, si: (0, 0)
    return pl.pallas_call(
        _merge_kernel,
        grid=(b, s // tm),
        in_specs=[pl.BlockSpec((1, tm, d), tok),
                  pl.BlockSpec((1, tm, SB_WIDTH), tok),
                  pl.BlockSpec((1, tm, RET_V_WIDTH), tok),
                  pl.BlockSpec((1, tm, d), lambda bi, si: (bi, si, OFF_BR // d)),
                  pl.BlockSpec((1, tm, d), lambda bi, si: (bi, si, OFF_BR // d + 1)),
                  pl.BlockSpec((1, 2 * d), zero2),
                  pl.BlockSpec((1, 1, 1, d), lambda bi, si: (bi, 2, 0, 0)),
                  _const_spec((SB_WIDTH, d), zero2),
                  _const_spec((RET_V_WIDTH, d), zero2),
                  _const_spec((d, d), zero2)],
        out_specs=pl.BlockSpec((1, tm, d), tok),
        out_shape=jax.ShapeDtypeStruct((b, s, d), F32),
        compiler_params=_cparams(("arbitrary", "arbitrary")),
        name="merge",
    )(x, o_sb, o_ret, proj, proj, b_branch.reshape(1, 2 * d), mod4, w_sb, w_ret, w_out)


def _ffn_kernel(x_ref, g2_ref, shift_ref, scale_ref, gate_ref, gf_ref, wa_ref, wu_ref, wo_ref, o_ref,
                *, final_norm):
    x = x_ref[0]
    y = x * lax.rsqrt(jnp.mean(x * x, axis=-1, keepdims=True) + EPS)
    y = y * g2_ref[...]
    h = (y * (1.0 + scale_ref[0, 0]) + shift_ref[0, 0]).astype(BF16)
    a = jnp.dot(h, wa_ref[...], preferred_element_type=F32)
    u = jnp.dot(h, wu_ref[...], preferred_element_type=F32)
    t = (a * jax.nn.sigmoid(a) * u).astype(BF16)
    x2 = x + gate_ref[0, 0] * jnp.dot(t, wo_ref[...], preferred_element_type=F32)
    if final_norm:
        r = x2 * lax.rsqrt(jnp.mean(x2 * x2, axis=-1, keepdims=True) + EPS)
        x2 = r * gf_ref[...]
    o_ref[0] = x2


def _ffn(x, g_norm2, mod4, g_final, w_a, w_u, w_o, final_norm):
    b, s, d = x.shape
    tm = TM_OUT
    tok = lambda bi, si: (bi, si, 0)
    zero2 = lambda bi, si: (0, 0)
    return pl.pallas_call(
        functools.partial(_ffn_kernel, final_norm=final_norm),
        grid=(b, s // tm),
        in_specs=[pl.BlockSpec((1, tm, d), tok),
                  pl.BlockSpec((1, d), zero2),
                  pl.BlockSpec((1, 1, 1, d), lambda bi, si: (bi, 3, 0, 0)),
                  pl.BlockSpec((1, 1, 1, d), lambda bi, si: (bi, 4, 0, 0)),
                  pl.BlockSpec((1, 1, 1, d), lambda bi, si: (bi, 5, 0, 0)),
                  pl.BlockSpec((1, d), zero2),
                  _const_spec((d, D_FF), zero2),
                  _const_spec((d, D_FF), zero2),
                  _const_spec((D_FF, d), zero2)],
        out_specs=pl.BlockSpec((1, tm, d), tok),
        out_shape=jax.ShapeDtypeStruct((b, s, d), F32),
        compiler_params=_cparams(("arbitrary", "arbitrary")),
        name="ffn",
    )(x, g_norm2.reshape(1, d), mod4, mod4, mod4, g_final.reshape(1, d), w_a, w_u, w_o)


def kernel(x, c, w_ada, b_ada, g_norm1, w_in, b_branch, w_proj_sb, w_proj_ret, w_out,
           g_norm2, w_ffn_in, w_ffn_out, g_final):
    b, s, d = x.shape
    depth = w_ada.shape[0]
    cos, sin = _rope_tables(s)
    log_gamma = jnp.log(1.0 - 2.0 ** (-5.0 - jnp.arange(RET_HEADS, dtype=jnp.float32)))
    for l in range(depth):
        mod4 = _ada(c, w_ada[l], b_ada[l]).reshape(b, N_MOD, 1, d)
        proj = _inproj(x, g_norm1[l], mod4, w_in[l].astype(BF16), cos, sin)
        o_sb = _sb_attention(proj)
        o_ret = _retention(proj, log_gamma)
        x = _merge(x, o_sb, o_ret, proj, b_branch[l], mod4,
                   w_proj_sb[l].astype(BF16), w_proj_ret[l].astype(BF16), w_out[l].astype(BF16))
        w_ffn = w_ffn_in[l].astype(BF16)
        x = _ffn(x, g_norm2[l], mod4, g_final, w_ffn[:, :D_FF], w_ffn[:, D_FF:],
                 w_ffn_out[l].astype(BF16), final_norm=(l == depth - 1))
    return x
```

```python
import functools

import jax
import jax.numpy as jnp
from jax import lax
from jax.experimental import pallas as pl
from jax.experimental.pallas import tpu as pltpu

D_MODEL = 1024
SB_HEADS = 16
SB_HEAD_DIM = 64
SB_WIDTH = SB_HEADS * SB_HEAD_DIM
RET_HEADS = 4
RET_QK_DIM = 256
RET_V_DIM = 512
RET_QK_WIDTH = RET_HEADS * RET_QK_DIM
RET_V_WIDTH = RET_HEADS * RET_V_DIM
D_FF = 2816
ROPE_BASE = 10000.0
EPS = 1e-6
N_MOD = 6
LOG2E = 1.4426950408889634

LANE = 128
OFF_SB_Q = 0
OFF_SB_K = SB_WIDTH
OFF_SB_V = 2 * SB_WIDTH
OFF_R_Q = 3 * SB_WIDTH
OFF_R_K = OFF_R_Q + RET_QK_WIDTH
OFF_R_V = OFF_R_K + RET_QK_WIDTH
OFF_R_G = OFF_R_V + RET_V_WIDTH
OFF_BR = OFF_R_G + RET_V_WIDTH
IN_WIDTH = OFF_BR + 2 * D_MODEL

F32 = jnp.float32
BF16 = jnp.bfloat16

VMEM_LIMIT = 56 * 1024 * 1024

TM_IN = 2048
TN_IN = 1024
SB_TQ = 256
SB_TK = 256
SB_G = 4
RET_C = 256
TM_OUT = 512


def _cparams(sem):
    return pltpu.CompilerParams(dimension_semantics=sem, vmem_limit_bytes=VMEM_LIMIT)


def _const_spec(shape, index_map):
    return pl.BlockSpec(shape, index_map, pipeline_mode=pl.Buffered(1))


def _ada_kernel(c_ref, w_ref, b_ref, o_ref):
    c = c_ref[...]
    c_act = (c * jax.nn.sigmoid(c)).astype(BF16)
    o_ref[...] = jnp.dot(c_act, w_ref[...].astype(BF16),
                         preferred_element_type=F32) + b_ref[...]


def _ada(c, w_ada, b_ada):
    b, d = c.shape
    n = w_ada.shape[1]
    tn = 1024
    return pl.pallas_call(
        _ada_kernel,
        grid=(n // tn,),
        in_specs=[pl.BlockSpec((b, d), lambda j: (0, 0)),
                  pl.BlockSpec((d, tn), lambda j: (0, j)),
                  pl.BlockSpec((1, tn), lambda j: (0, j))],
        out_specs=pl.BlockSpec((b, tn), lambda j: (0, j)),
        out_shape=jax.ShapeDtypeStruct((b, n), F32),
        compiler_params=_cparams(("arbitrary",)),
        name="ada",
    )(c, w_ada, b_ada.reshape(1, n))


def _rope_kernel(cos_ref, sin_ref):
    s, half = cos_ref.shape
    i = lax.broadcasted_iota(jnp.int32, (s, half), 1).astype(F32)
    pos = lax.broadcasted_iota(jnp.int32, (s, half), 0).astype(F32)
    inv_freq = jnp.power(ROPE_BASE, -(2.0 * i) / RET_QK_DIM)
    ang = pos * inv_freq
    cos_ref[...] = jnp.cos(ang)
    sin_ref[...] = jnp.sin(ang)


def _rope_tables(s_len):
    half = RET_QK_DIM // 2
    shp = jax.ShapeDtypeStruct((s_len, half), F32)
    return pl.pallas_call(
        _rope_kernel,
        out_shape=(shp, shp),
        compiler_params=pltpu.CompilerParams(vmem_limit_bytes=VMEM_LIMIT),
        name="rope",
    )()


def _rotate_heads(t, cos, sin):
    half = RET_QK_DIM // 2
    parts = []
    for h in range(t.shape[1] // RET_QK_DIM):
        t1 = t[:, h * RET_QK_DIM:h * RET_QK_DIM + half]
        t2 = t[:, h * RET_QK_DIM + half:(h + 1) * RET_QK_DIM]
        parts += [t1 * cos - t2 * sin, t1 * sin + t2 * cos]
    return jnp.concatenate(parts, axis=-1)


def _inproj_kernel(x_ref, g_ref, shift_ref, scale_ref, w_ref, cos_ref, sin_ref, o_ref, h_ref):
    ni = pl.program_id(2)

    @pl.when(ni == 0)
    def _():
        x = x_ref[0]
        y = x * lax.rsqrt(jnp.mean(x * x, axis=-1, keepdims=True) + EPS)
        y = y * g_ref[...]
        h = y * (1.0 + scale_ref[0, 0]) + shift_ref[0, 0]
        h_ref[...] = h.astype(BF16)

    def proj():
        return jnp.dot(h_ref[...], w_ref[...], preferred_element_type=F32)

    is_sbq = ni == OFF_SB_Q // TN_IN
    is_q = ni == OFF_R_Q // TN_IN
    is_k = ni == OFF_R_K // TN_IN
    is_g = jnp.logical_and(ni >= OFF_R_G // TN_IN, ni < OFF_BR // TN_IN)

    @pl.when(is_sbq)
    def _():
        o_ref[0] = (proj() * (SB_HEAD_DIM ** -0.5 * LOG2E)).astype(BF16)

    @pl.when(is_q)
    def _():
        o_ref[0] = _rotate_heads(proj(), cos_ref[...], sin_ref[...]).astype(BF16)

    @pl.when(is_k)
    def _():
        kr = _rotate_heads(proj(), cos_ref[...], sin_ref[...]) * (RET_QK_DIM ** -0.5)
        o_ref[0] = kr.astype(BF16)

    @pl.when(is_g)
    def _():
        a = proj()
        o_ref[0] = (a * jax.nn.sigmoid(a)).astype(BF16)

    @pl.when(jnp.logical_not(is_sbq | is_q | is_k | is_g))
    def _():
        o_ref[0] = proj().astype(BF16)


def _inproj(x, g_norm1, mod4, w_in_bf16, cos, sin):
    b, s, d = x.shape
    n = w_in_bf16.shape[1]
    half = RET_QK_DIM // 2
    assert OFF_R_Q % TN_IN == 0 and RET_QK_WIDTH == TN_IN and OFF_R_G % TN_IN == 0 and OFF_BR % TN_IN == 0
    assert SB_WIDTH == TN_IN
    return pl.pallas_call(
        _inproj_kernel,
        grid=(b, s // TM_IN, n // TN_IN),
        in_specs=[pl.BlockSpec((1, TM_IN, d), lambda bi, si, ni: (bi, si, 0)),
                  pl.BlockSpec((1, d), lambda bi, si, ni: (0, 0)),
                  pl.BlockSpec((1, 1, 1, d), lambda bi, si, ni: (bi, 0, 0, 0)),
                  pl.BlockSpec((1, 1, 1, d), lambda bi, si, ni: (bi, 1, 0, 0)),
                  pl.BlockSpec((d, TN_IN), lambda bi, si, ni: (0, ni)),
                  pl.BlockSpec((TM_IN, half), lambda bi, si, ni: (si, 0)),
                  pl.BlockSpec((TM_IN, half), lambda bi, si, ni: (si, 0))],
        out_specs=pl.BlockSpec((1, TM_IN, TN_IN), lambda bi, si, ni: (bi, si, ni)),
        out_shape=jax.ShapeDtypeStruct((b, s, n), BF16),
        scratch_shapes=[pltpu.VMEM((TM_IN, d), BF16)],
        compiler_params=_cparams(("arbitrary", "arbitrary", "arbitrary")),
        name="inproj",
    )(x, g_norm1.reshape(1, d), mod4, mod4, w_in_bf16, cos, sin)


SB_DONE = 128.0
SB_EXP_CLAMP = 126.0


def _sb_kernel(q_ref, k_ref, v_ref, o_ref, carry_ref, acc_ref, *, groups):
    qi = pl.program_id(2)
    tq, tk, hd = SB_TQ, SB_TK, SB_HEAD_DIM

    lane_q = lax.broadcasted_iota(jnp.int32, (tq, LANE), 1)
    ur = lax.broadcasted_iota(jnp.int32, (tk, tk), 0)
    uc = lax.broadcasted_iota(jnp.int32, (tk, tk), 1)
    u = jnp.where(ur > uc, 1.0, 0.0).astype(BF16)
    lane_v = lax.broadcasted_iota(jnp.int32, (tk, LANE), 1)
    causal = uc < ur
    chains = [(g, h) for g in range(groups) for h in range(2)]

    def head_q(g, h):
        q = q_ref[0, :, g * LANE:(g + 1) * LANE]
        keep = (lane_q < hd) if h == 0 else (lane_q >= hd)
        return jnp.where(keep, q, jnp.zeros_like(q))

    qh = {c: head_q(*c) for c in chains}

    def softplus2(z2, diag):
        sp = jnp.maximum(z2, jnp.log(1.0 + jnp.exp2(jnp.minimum(z2, SB_EXP_CLAMP))) * LOG2E)
        return (jnp.where(causal, sp, 0.0) if diag else sp), z2 - sp

    def weights(ls, carry, cum, diag):
        a = jnp.exp2(ls - carry - cum)
        if diag:
            a = jnp.where(causal, a, 0.0)
        return a.astype(BF16)

    def tile(kb, diag):
        start = pl.multiple_of(kb * tk, tk)
        k = [k_ref[0, pl.ds(start, tk), g * LANE:(g + 1) * LANE] for g in range(groups)]
        z2 = {c: lax.dot_general(qh[c], k[c[0]], (((1,), (1,)), ((), ())),
                                 preferred_element_type=F32) for c in chains}
        spls = {c: softplus2(z2[c], diag) for c in chains}
        cum = {c: jnp.dot(spls[c][0].astype(BF16), u, preferred_element_type=F32) for c in chains}
        a = {c: weights(spls[c][1], carry_ref[c], cum[c], diag) for c in chains}
        mass = []
        for g in range(groups):
            v = v_ref[0, pl.ds(start, tk), g * LANE:(g + 1) * LANE]
            zv = jnp.zeros_like(v)
            vv = jnp.concatenate([jnp.where(lane_v < hd, v, zv),
                                  jnp.where(lane_v >= hd, v, zv)], axis=0)
            aa = jnp.concatenate([a[(g, 0)], a[(g, 1)]], axis=1)
            acc_ref[g] += jnp.dot(aa, vv, preferred_element_type=F32)
        for c in chains:
            carry = carry_ref[c] + jnp.sum(spls[c][0], axis=-1, keepdims=True)
            carry_ref[c] = carry
            mass.append(carry)
        return jnp.min(functools.reduce(jnp.minimum, mass))

    carry_ref[...] = jnp.zeros_like(carry_ref)
    acc_ref[...] = jnp.zeros_like(acc_ref)

    def cond(c):
        return jnp.logical_and(c[0] < qi, c[1] < SB_DONE)

    def body(c):
        return c[0] + 1, tile(qi - 1 - c[0], False)

    lax.while_loop(cond, body, (jnp.int32(0), tile(qi, True)))
    for g in range(groups):
        o_ref[0, :, g * LANE:(g + 1) * LANE] = acc_ref[g].astype(BF16)


def _sb_attention(proj):
    b, s, _ = proj.shape
    w = SB_G * LANE
    return pl.pallas_call(
        functools.partial(_sb_kernel, groups=SB_G),
        grid=(b, SB_WIDTH // w, s // SB_TQ),
        in_specs=[pl.BlockSpec((1, SB_TQ, w), lambda bi, hg, qi: (bi, qi, OFF_SB_Q // w + hg)),
                  pl.BlockSpec((1, s, w), lambda bi, hg, qi: (bi, 0, OFF_SB_K // w + hg)),
                  pl.BlockSpec((1, s, w), lambda bi, hg, qi: (bi, 0, OFF_SB_V // w + hg))],
        out_specs=pl.BlockSpec((1, SB_TQ, w), lambda bi, hg, qi: (bi, qi, hg)),
        out_shape=jax.ShapeDtypeStruct((b, s, SB_WIDTH), BF16),
        scratch_shapes=[pltpu.VMEM((SB_G, 2, SB_TQ, 1), F32),
                        pltpu.VMEM((SB_G, SB_TQ, LANE), F32)],
        compiler_params=_cparams(("arbitrary", "arbitrary", "arbitrary")),
        name="sb",
    )(proj, proj, proj)


def _ret_kernel(lg_ref, q_ref, k_ref, v01_ref, v23_ref, sg01_ref, sg23_ref, o_ref,
                state_ref, dmask_ref, qdec_ref, kdec_ref):
    n = pl.program_id(1)
    c = RET_C
    dk, dv = RET_QK_DIM, RET_V_DIM
    v_refs = (v01_ref, v23_ref)
    sg_refs = (sg01_ref, sg23_ref)

    @pl.when(n == 0)
    def _():
        state_ref[...] = jnp.zeros_like(state_ref)
        r = lax.broadcasted_iota(jnp.int32, (c, c), 0)
        cc = lax.broadcasted_iota(jnp.int32, (c, c), 1)
        diff = (r - cc).astype(F32)
        idx = lax.broadcasted_iota(jnp.int32, (c, 1), 0).astype(F32)
        for h in range(RET_HEADS):
            lg = lg_ref[h]
            dmask_ref[h] = jnp.where(diff >= 0.0, jnp.exp(jnp.maximum(diff, 0.0) * lg), 0.0)
            qdec_ref[h] = jnp.exp((idx + 1.0) * lg)
            kdec_ref[h] = jnp.exp((c - 1.0 - idx) * lg)

    heads = range(RET_HEADS)
    vcol = [slice((h % 2) * dv, (h % 2 + 1) * dv) for h in heads]
    q = [q_ref[0, :, h * dk:(h + 1) * dk] for h in heads]
    k = [k_ref[0, :, h * dk:(h + 1) * dk] for h in heads]
    v = [v_refs[h // 2][0, :, vcol[h]] for h in heads]
    scores = [lax.dot_general(q[h], k[h], (((1,), (1,)), ((), ())),
                              preferred_element_type=F32) for h in heads]
    state = [state_ref[h] for h in heads]
    cross = [jnp.dot(q[h], state[h].astype(BF16), preferred_element_type=F32) for h in heads]
    intra = [jnp.dot((scores[h] * dmask_ref[h]).astype(BF16), v[h],
                     preferred_element_type=F32) for h in heads]
    kd = [(k[h].astype(F32) * kdec_ref[h]).astype(BF16) for h in heads]
    upd = [lax.dot_general(kd[h], v[h], (((0,), (0,)), ((), ())),
                           preferred_element_type=F32) for h in heads]
    for h in heads:
        chunk_decay = jnp.exp(jnp.full((1, dv), c * lg_ref[h], F32))
        state_ref[h] = state[h] * chunk_decay + upd[h]
        out = intra[h] + cross[h] * qdec_ref[h]
        out = out * lax.rsqrt(jnp.mean(out * out, axis=-1, keepdims=True) + EPS)
        sg = sg_refs[h // 2][0, :, vcol[h]].astype(F32)
        o_ref[0, :, h * dv:(h + 1) * dv] = (sg * out).astype(BF16)


def _retention(proj, log_gamma):
    b, s, _ = proj.shape
    c = RET_C
    qw, vw = RET_QK_WIDTH, RET_V_WIDTH
    hw = 2 * RET_V_DIM
    return pl.pallas_call(
        _ret_kernel,
        grid=(b, s // c),
        in_specs=[pl.BlockSpec(memory_space=pltpu.SMEM),
                  pl.BlockSpec((1, c, qw), lambda bi, n: (bi, n, OFF_R_Q // qw)),
                  pl.BlockSpec((1, c, qw), lambda bi, n: (bi, n, OFF_R_K // qw)),
                  pl.BlockSpec((1, c, hw), lambda bi, n: (bi, n, OFF_R_V // hw)),
                  pl.BlockSpec((1, c, hw), lambda bi, n: (bi, n, OFF_R_V // hw + 1)),
                  pl.BlockSpec((1, c, hw), lambda bi, n: (bi, n, OFF_R_G // hw)),
                  pl.BlockSpec((1, c, hw), lambda bi, n: (bi, n, OFF_R_G // hw + 1))],
        out_specs=pl.BlockSpec((1, c, vw), lambda bi, n: (bi, n, 0)),
        out_shape=jax.ShapeDtypeStruct((b, s, vw), BF16),
        scratch_shapes=[pltpu.VMEM((RET_HEADS, RET_QK_DIM, RET_V_DIM), F32),
                        pltpu.VMEM((RET_HEADS, c, c), F32),
                        pltpu.VMEM((RET_HEADS, c, 1), F32),
                        pltpu.VMEM((RET_HEADS, c, 1), F32)],
        compiler_params=_cparams(("arbitrary", "arbitrary")),
        name="ret",
    )(log_gamma, proj, proj, proj, proj, proj, proj)


def _merge_kernel(x_ref, osb_ref, oret_ref, brsb_ref, brret_ref, bb_ref, gate_ref,
                  wsb_ref, wret_ref, wout_ref, o_ref):
    d = D_MODEL
    p_sb = jnp.dot(osb_ref[0], wsb_ref[...], preferred_element_type=F32)
    p_ret = jnp.dot(oret_ref[0], wret_ref[...], preferred_element_type=F32)
    g_sb = jax.nn.sigmoid(brsb_ref[0].astype(F32) + bb_ref[:, :d])
    g_ret = jax.nn.sigmoid(brret_ref[0].astype(F32) + bb_ref[:, d:])
    merged = (g_sb * p_sb + g_ret * p_ret).astype(BF16)
    o_ref[0] = x_ref[0] + gate_ref[0, 0] * jnp.dot(merged, wout_ref[...],
                                                  preferred_element_type=F32)


def _merge(x, o_sb, o_ret, proj, b_branch, mod4, w_sb, w_ret, w_out):
    b, s, d = x.shape
    tm = TM_OUT
    tok = lambda bi, si: (bi, si, 0)
    zero2 = lambda bi, si: (0, 0)
    return pl.pallas_call(
        _merge_kernel,
        grid=(b, s // tm),
        in_specs=[pl.BlockSpec((1, tm, d), tok),
                  pl.BlockSpec((1, tm, SB_WIDTH), tok),
                  pl.BlockSpec((1, tm, RET_V_WIDTH), tok),
                  pl.BlockSpec((1, tm, d), lambda bi, si: (bi, si, OFF_BR // d)),
                  pl.BlockSpec((1, tm, d), lambda bi, si: (bi, si, OFF_BR // d + 1)),
                  pl.BlockSpec((1, 2 * d), zero2),
                  pl.BlockSpec((1, 1, 1, d), lambda bi, si: (bi, 2, 0, 0)),
                  _const_spec((SB_WIDTH, d), zero2),
                  _const_spec((RET_V_WIDTH, d), zero2),
                  _const_spec((d, d), zero2)],
        out_specs=pl.BlockSpec((1, tm, d), tok),
        out_shape=jax.ShapeDtypeStruct((b, s, d), F32),
        compiler_params=_cparams(("arbitrary", "arbitrary")),
        name="merge",
    )(x, o_sb, o_ret, proj, proj, b_branch.reshape(1, 2 * d), mod4, w_sb, w_ret, w_out)


def _ffn_kernel(x_ref, g2_ref, shift_ref, scale_ref, gate_ref, gf_ref, wa_ref, wu_ref, wo_ref, o_ref,
                *, final_norm):
    x = x_ref[0]
    y = x * lax.rsqrt(jnp.mean(x * x, axis=-1, keepdims=True) + EPS)
    y = y * g2_ref[...]
    h = (y * (1.0 + scale_ref[0, 0]) + shift_ref[0, 0]).astype(BF16)
    a = jnp.dot(h, wa_ref[...], preferred_element_type=F32)
    u = jnp.dot(h, wu_ref[...], preferred_element_type=F32)
    t = (a * jax.nn.sigmoid(a) * u).astype(BF16)
    x2 = x + gate_ref[0, 0] * jnp.dot(t, wo_ref[...], preferred_element_type=F32)
    if final_norm:
        r = x2 * lax.rsqrt(jnp.mean(x2 * x2, axis=-1, keepdims=True) + EPS)
        x2 = r * gf_ref[...]
    o_ref[0] = x2


def _ffn(x, g_norm2, mod4, g_final, w_a, w_u, w_o, final_norm):
    b, s, d = x.shape
    tm = TM_OUT
    tok = lambda bi, si: (bi, si, 0)
    zero2 = lambda bi, si: (0, 0)
    return pl.pallas_call(
        functools.partial(_ffn_kernel, final_norm=final_norm),
        grid=(b, s // tm),
        in_specs=[pl.BlockSpec((1, tm, d), tok),
                  pl.BlockSpec((1, d), zero2),
                  pl.BlockSpec((1, 1, 1, d), lambda bi, si: (bi, 3, 0, 0)),
                  pl.BlockSpec((1, 1, 1, d), lambda bi, si: (bi, 4, 0, 0)),
                  pl.BlockSpec((1, 1, 1, d), lambda bi, si: (bi, 5, 0, 0)),
                  pl.BlockSpec((1, d), zero2),
                  _const_spec((d, D_FF), zero2),
                  _const_spec((d, D_FF), zero2),
                  _const_spec((D_FF, d), zero2)],
        out_specs=pl.BlockSpec((1, tm, d), tok),
        out_shape=jax.ShapeDtypeStruct((b, s, d), F32),
        compiler_params=_cparams(("arbitrary", "arbitrary")),
        name="ffn",
    )(x, g_norm2.reshape(1, d), mod4, mod4, mod4, g_final.reshape(1, d), w_a, w_u, w_o)


def kernel(x, c, w_ada, b_ada, g_norm1, w_in, b_branch, w_proj_sb, w_proj_ret, w_out,
           g_norm2, w_ffn_in, w_ffn_out, g_final):
    b, s, d = x.shape
    depth = w_ada.shape[0]
    cos, sin = _rope_tables(s)
    log_gamma = jnp.log(1.0 - 2.0 ** (-5.0 - jnp.arange(RET_HEADS, dtype=jnp.float32)))
    for l in range(depth):
        mod4 = _ada(c, w_ada[l], b_ada[l]).reshape(b, N_MOD, 1, d)
        proj = _inproj(x, g_norm1[l], mod4, w_in[l].astype(BF16), cos, sin)
        o_sb = _sb_attention(proj)
        o_ret = _retention(proj, log_gamma)
        x = _merge(x, o_sb, o_ret, proj, b_branch[l], mod4,
                   w_proj_sb[l].astype(BF16), w_proj_ret[l].astype(BF16), w_out[l].astype(BF16))
        w_ffn = w_ffn_in[l].astype(BF16)
        x = _ffn(x, g_norm2[l], mod4, g_final, w_ffn[:, :D_FF], w_ffn[:, D_FF:],
                 w_ffn_out[l].astype(BF16), final_norm=(l == depth - 1))
    return x
```
